```python
import jax, jax.numpy as jnp
from jax import lax
import numpy as np

D_MODEL = 1024
BATCH = 8
SEQ = 4096
DEPTH = 2

D_MIX = D_MODEL
ATTN_WIDTH = D_MIX // 2
REC_WIDTH = D_MIX - ATTN_WIDTH
HEAD_DIM = 64
N_HEADS = ATTN_WIDTH // HEAD_DIM
ROT_DIM = HEAD_DIM // 4
ROPE_THETA = 500000.0
MOBA_BLOCK = 256
MOBA_TOPK = 3
Q_CHUNK = 32
REC_BLOCKS = 8
REC_BLOCK = REC_WIDTH // REC_BLOCKS
CONV_WIDTH = 4
LRU_C = 8.0
N_GROUPS = 4
EXPERTS_PER_GROUP = 4
N_EXPERTS = N_GROUPS * EXPERTS_PER_GROUP
MOE_TOPK = 2
D_EXPERT = 512
IN_PROJ_WIDTH = 3 * ATTN_WIDTH + 2 * REC_WIDTH
DEEPNORM_ALPHA = (2 * DEPTH) ** 0.25
DEEPNORM_BETA = (8 * DEPTH) ** -0.25
LN_EPS = 1e-5
RMS_EPS = 1e-6

kernel_name = 'hymba_moba_rglru_hmoe_deepnorm'

F32 = jnp.float32


def _layer_norm(x, g, b):
    xf = x.astype(F32)
    mu = jnp.mean(xf, axis=-1, keepdims=True)
    var = jnp.mean(jnp.square(xf - mu), axis=-1, keepdims=True)
    return ((xf - mu) * lax.rsqrt(var + LN_EPS) * g.astype(F32) + b.astype(F32)).astype(x.dtype)


def _rms_norm(x, g):
    xf = x.astype(F32)
    return (xf * lax.rsqrt(jnp.mean(xf * xf, axis=-1, keepdims=True) + RMS_EPS) * g.astype(F32)).astype(x.dtype)


def _partial_rope(t, pos):
    half = ROT_DIM // 2
    inv_freq = ROPE_THETA ** (-jnp.arange(half, dtype=F32) * 2.0 / ROT_DIM)
    ang = pos.astype(F32)[:, None] * inv_freq[None, :]
    cos = jnp.cos(ang).astype(t.dtype)
    sin = jnp.sin(ang).astype(t.dtype)
    t1 = t[..., :half]
    t2 = t[..., half:ROT_DIM]
    rest = t[..., ROT_DIM:]
    return jnp.concatenate([t1 * cos - t2 * sin, t2 * cos + t1 * sin, rest], axis=-1)


def _moba_attention(q, k, v):
    B, H, S, Dh = q.shape
    nb = S // MOBA_BLOCK
    k_sel = min(MOBA_TOPK, nb)
    kb = k.reshape(B, H, nb, MOBA_BLOCK, Dh)
    vb = v.reshape(B, H, nb, MOBA_BLOCK, Dh)
    kmean = jnp.mean(kb.astype(F32), axis=3).astype(k.dtype)
    scale = HEAD_DIM ** -0.5
    b_ix = jnp.arange(B)[:, None, None, None]
    h_ix = jnp.arange(H)[None, :, None, None]
    blk_ids = jnp.arange(nb)
    rank = jnp.arange(k_sel)

    def chunk(c):
        q0 = c * Q_CHUNK
        bi = q0 // MOBA_BLOCK
        qc = lax.dynamic_slice_in_dim(q, q0, Q_CHUNK, axis=2)
        qpos = q0 + jnp.arange(Q_CHUNK)
        k_own = lax.dynamic_slice_in_dim(k, bi * MOBA_BLOCK, MOBA_BLOCK, axis=2)
        v_own = lax.dynamic_slice_in_dim(v, bi * MOBA_BLOCK, MOBA_BLOCK, axis=2)
        kpos = bi * MOBA_BLOCK + jnp.arange(MOBA_BLOCK)
        s_own = jnp.einsum('bhqd,bhkd->bhqk', qc, k_own).astype(F32) * scale
        s_own = jnp.where(kpos[None, :] <= qpos[:, None], s_own, -jnp.inf)
        gate = jnp.einsum('bhqd,bhnd->bhqn', qc, kmean).astype(F32)
        gate = jnp.where(blk_ids < bi, gate, -jnp.inf)
        _, sel = lax.top_k(gate, k_sel)
        k_g = kb[b_ix, h_ix, sel]
        v_g = vb[b_ix, h_ix, sel]
        s_past = jnp.einsum('bhqd,bhqjkd->bhqjk', qc, k_g).astype(F32) * scale
        s_past = jnp.where((rank < bi)[:, None], s_past, -jnp.inf)
        s_past = s_past.reshape(B, H, Q_CHUNK, k_sel * MOBA_BLOCK)
        p = jax.nn.softmax(jnp.concatenate([s_past, s_own], axis=-1), axis=-1).astype(v.dtype)
        p_past = p[..., :k_sel * MOBA_BLOCK].reshape(B, H, Q_CHUNK, k_sel, MOBA_BLOCK)
        p_own = p[..., k_sel * MOBA_BLOCK:]
        return (jnp.einsum('bhqjk,bhqjkd->bhqd', p_past, v_g)
                + jnp.einsum('bhqk,bhkd->bhqd', p_own, v_own))

    out = lax.map(chunk, jnp.arange(S // Q_CHUNK))
    return jnp.moveaxis(out, 0, 2).reshape(B, H, S, Dh)


def _rglru_branch(xr, gr, conv_w, conv_b, w_a, b_a, w_i, b_i, lam):
    B, S, R = xr.shape
    xc = lax.conv_general_dilated(
        xr, conv_w[:, None, :], window_strides=(1,), padding=[(CONV_WIDTH - 1, 0)],
        dimension_numbers=('NWC', 'WIO', 'NWC'), feature_group_count=R) + conv_b
    xb = xc.reshape(B, S, REC_BLOCKS, REC_BLOCK)
    r = jax.nn.sigmoid(jnp.einsum('bsnc,ncd->bsnd', xb, w_a).reshape(B, S, R) + b_a)
    i = jax.nn.sigmoid(jnp.einsum('bsnc,ncd->bsnd', xb, w_i).reshape(B, S, R) + b_i)
    log_a = (-LRU_C * r.astype(F32)) * jax.nn.softplus(-lam.astype(F32))
    a = jnp.exp(log_a)
    u = jnp.sqrt(-jnp.expm1(2.0 * log_a)) * (i * xc).astype(F32)

    def combine(e1, e2):
        a1, b1 = e1
        a2, b2 = e2
        return a1 * a2, a2 * b1 + b2

    _, h = lax.associative_scan(combine, (a, u), axis=1)
    return h.astype(xr.dtype) * jax.nn.gelu(gr)


def _hybrid_mixer(x, w_in, conv_w, conv_b, w_rg_a, b_rg_a, w_rg_i, b_rg_i, lru_lambda,
                  g_attn_norm, g_rec_norm, w_out):
    B, S, _ = x.shape
    proj = x @ w_in
    q, k, v, xr, gr = jnp.split(
        proj, [ATTN_WIDTH, 2 * ATTN_WIDTH, 3 * ATTN_WIDTH, 3 * ATTN_WIDTH + REC_WIDTH], axis=-1)
    s_pad = -(-S // MOBA_BLOCK) * MOBA_BLOCK

    def heads(t):
        t = t.reshape(B, S, N_HEADS, HEAD_DIM).transpose(0, 2, 1, 3)
        return jnp.pad(t, ((0, 0), (0, 0), (0, s_pad - S), (0, 0)))

    pos = jnp.arange(s_pad)
    qh = _partial_rope(heads(q), pos)
    kh = _partial_rope(heads(k), pos)
    vh = heads(v)
    attn = _moba_attention(qh, kh, vh)[:, :, :S].transpose(0, 2, 1, 3).reshape(B, S, ATTN_WIDTH)
    rec = _rglru_branch(xr, gr, conv_w, conv_b, w_rg_a, b_rg_a, w_rg_i, b_rg_i, lru_lambda)
    mixed = jnp.concatenate([_rms_norm(attn, g_attn_norm), _rms_norm(rec, g_rec_norm)], axis=-1)
    return mixed @ w_out


def _hier_moe(x, w_router_group, b_router_group, w_router_expert, b_router_expert,
              w_gate, w_up, w_down):
    B, S, D = x.shape
    xt = x.reshape(-1, D)
    T = xt.shape[0]
    g_prob = jax.nn.softmax((xt @ w_router_group).astype(F32) + b_router_group.astype(F32), axis=-1)
    g_p, g_idx = lax.top_k(g_prob, 1)
    e_logits = ((xt @ w_router_expert).astype(F32) + b_router_expert.astype(F32)).reshape(
        T, N_GROUPS, EXPERTS_PER_GROUP)
    e_logits = e_logits[jnp.arange(T), g_idx[:, 0]]
    e_p, e_idx = lax.top_k(jax.nn.softmax(e_logits, axis=-1), MOE_TOPK)
    wts = g_p * e_p / jnp.sum(e_p, axis=-1, keepdims=True)
    eid = g_idx * EXPERTS_PER_GROUP + e_idx
    comb = jnp.sum(jax.nn.one_hot(eid, N_EXPERTS, dtype=F32) * wts[..., None], axis=1).astype(x.dtype)
    y = jnp.zeros_like(xt)
    for e in range(N_EXPERTS):
        h = jax.nn.silu(xt @ w_gate[e]) * (xt @ w_up[e])
        y = y + comb[:, e:e + 1] * (h @ w_down[e])
    return y.reshape(B, S, D)


def setup_inputs(seed: int = 0) -> dict:
    key = jax.random.key(seed)
    ks = jax.random.split(key, 24)
    nrm = lambda k, shape, s: jax.random.normal(k, shape, F32) * s
    a_c = jax.random.uniform(ks[8], (DEPTH, REC_WIDTH), F32, minval=0.9, maxval=0.999)
    s_lam = a_c ** (1.0 / LRU_C)
    return {
        'x': nrm(ks[0], (BATCH, SEQ, D_MODEL), 1.0),
        'w_in': nrm(ks[1], (DEPTH, D_MODEL, IN_PROJ_WIDTH), D_MODEL ** -0.5),
        'conv_w': nrm(ks[2], (DEPTH, CONV_WIDTH, REC_WIDTH), CONV_WIDTH ** -0.5),
        'conv_b': nrm(ks[3], (DEPTH, REC_WIDTH), 0.01),
        'w_rg_a': nrm(ks[4], (DEPTH, REC_BLOCKS, REC_BLOCK, REC_BLOCK), REC_BLOCK ** -0.5),
        'b_rg_a': nrm(ks[5], (DEPTH, REC_WIDTH), 0.01),
        'w_rg_i': nrm(ks[6], (DEPTH, REC_BLOCKS, REC_BLOCK, REC_BLOCK), REC_BLOCK ** -0.5),
        'b_rg_i': nrm(ks[7], (DEPTH, REC_WIDTH), 0.01),
        'lru_lambda': jnp.log(s_lam) - jnp.log1p(-s_lam),
        'g_attn_norm': 1.0 + nrm(ks[9], (DEPTH, ATTN_WIDTH), 0.02),
        'g_rec_norm': 1.0 + nrm(ks[10], (DEPTH, REC_WIDTH), 0.02),
        'w_out': nrm(ks[11], (DEPTH, D_MIX, D_MODEL), D_MIX ** -0.5 * DEEPNORM_BETA),
        'ln1_g': 1.0 + nrm(ks[12], (DEPTH, D_MODEL), 0.02),
        'ln1_b': nrm(ks[13], (DEPTH, D_MODEL), 0.01),
        'w_router_group': nrm(ks[14], (DEPTH, D_MODEL, N_GROUPS), D_MODEL ** -0.5),
        'b_router_group': nrm(ks[15], (DEPTH, N_GROUPS), 0.01),
        'w_router_expert': nrm(ks[16], (DEPTH, D_MODEL, N_EXPERTS), D_MODEL ** -0.5),
        'b_router_expert': nrm(ks[17], (DEPTH, N_EXPERTS), 0.01),
        'w_gate': nrm(ks[18], (DEPTH, N_EXPERTS, D_MODEL, D_EXPERT), D_MODEL ** -0.5),
        'w_up': nrm(ks[19], (DEPTH, N_EXPERTS, D_MODEL, D_EXPERT), D_MODEL ** -0.5),
        'w_down': nrm(ks[20], (DEPTH, N_EXPERTS, D_EXPERT, D_MODEL), D_EXPERT ** -0.5 * DEEPNORM_BETA),
        'ln2_g': 1.0 + nrm(ks[21], (DEPTH, D_MODEL), 0.02),
        'ln2_b': nrm(ks[22], (DEPTH, D_MODEL), 0.01),
    }


def reference(x, w_in, conv_w, conv_b, w_rg_a, b_rg_a, w_rg_i, b_rg_i, lru_lambda,
              g_attn_norm, g_rec_norm, w_out, ln1_g, ln1_b, w_router_group, b_router_group,
              w_router_expert, b_router_expert, w_gate, w_up, w_down, ln2_g, ln2_b):
    for l in range(DEPTH):
        h = _hybrid_mixer(x, w_in[l], conv_w[l], conv_b[l], w_rg_a[l], b_rg_a[l], w_rg_i[l],
                          b_rg_i[l], lru_lambda[l], g_attn_norm[l], g_rec_norm[l], w_out[l])
        x = _layer_norm(DEEPNORM_ALPHA * x + h, ln1_g[l], ln1_b[l])
        h = _hier_moe(x, w_router_group[l], b_router_group[l], w_router_expert[l],
                      b_router_expert[l], w_gate[l], w_up[l], w_down[l])
        x = _layer_norm(DEEPNORM_ALPHA * x + h, ln2_g[l], ln2_b[l])
    return x
```

```python
import functools
import math

import jax
import jax.numpy as jnp
from jax import lax
from jax.experimental import pallas as pl
from jax.experimental.pallas import tpu as pltpu

F32 = jnp.float32
BF16 = jnp.bfloat16

HEAD_DIM = 64
ROT_DIM = HEAD_DIM // 4
ROPE_THETA = 500000.0
MOBA_BLOCK = 256
MOBA_TOPK = 3
CONV_WIDTH = 4
LRU_C = 8.0
N_GROUPS = 4
EXPERTS_PER_GROUP = 4
LN_EPS = 1e-5
RMS_EPS = 1e-6

LANES = 128
NEG_BIG = -1e30
LOG2E = 1.4426950408889634
Q_SCALE = HEAD_DIM ** -0.5 * LOG2E
VT_ROWS = HEAD_DIM + 16
VMEM_LIMIT = 56 * 1024 * 1024


def _cparams(sem):
    return pltpu.CompilerParams(dimension_semantics=sem, vmem_limit_bytes=VMEM_LIMIT)


def _inproj_kernel(x_ref, w_ref, c_ref, s1_ref, s2_ref, q_ref, k_ref, v_ref, rec_ref, *, aw):
    xb = x_ref[...].astype(BF16)
    cos, sin_lo, sin_hi = c_ref[...], s1_ref[...], s2_ref[...]

    def rope(t):
        outs = []
        for c in range(aw // LANES):
            tc = t[:, c * LANES:(c + 1) * LANES]
            outs.append(tc * cos + pltpu.roll(tc, 8, 1) * sin_lo + pltpu.roll(tc, LANES - 8, 1) * sin_hi)
        return jnp.concatenate(outs, axis=1)

    q = jnp.dot(xb, w_ref[:, 0:aw], preferred_element_type=F32)
    q_ref[...] = (rope(q) * Q_SCALE).astype(BF16)
    k = jnp.dot(xb, w_ref[:, aw:2 * aw], preferred_element_type=F32)
    k_ref[...] = rope(k).astype(BF16)
    v = jnp.dot(xb, w_ref[:, 2 * aw:3 * aw], preferred_element_type=F32)
    v_ref[...] = v.astype(BF16)
    rec_ref[...] = jnp.dot(xb, w_ref[:, 3 * aw:], preferred_element_type=F32)


def _rope_tables(seq):
    half = ROT_DIM // 2
    inv_freq = ROPE_THETA ** (-jnp.arange(half, dtype=F32) * 2.0 / ROT_DIM)
    ang = jnp.arange(seq).astype(F32)[:, None] * inv_freq[None, :]
    cos, sin = jnp.cos(ang), jnp.sin(ang)
    zeros = jnp.zeros((seq, HEAD_DIM - ROT_DIM), F32)
    zh = jnp.zeros((seq, half), F32)
    c64 = jnp.concatenate([cos, cos, zeros + 1.0], axis=1)
    lo64 = jnp.concatenate([zh, sin, zeros], axis=1)
    hi64 = jnp.concatenate([-sin, zh, zeros], axis=1)
    rep = LANES // HEAD_DIM
    return jnp.tile(c64, (1, rep)), jnp.tile(lo64, (1, rep)), jnp.tile(hi64, (1, rep))


def _in_proj(x2, w_bf, tables, seq, aw):
    t, d = x2.shape
    n = w_bf.shape[1]
    tm = min(512, seq)
    nseq = seq // tm
    tab_spec = pl.BlockSpec((tm, LANES), lambda i: (i % nseq, 0))
    return pl.pallas_call(
        functools.partial(_inproj_kernel, aw=aw),
        grid=(t // tm,),
        in_specs=[pl.BlockSpec((tm, d), lambda i: (i, 0)),
                  pl.BlockSpec((d, n), lambda i: (0, 0)),
                  tab_spec, tab_spec, tab_spec],
        out_specs=[pl.BlockSpec((tm, aw), lambda i: (i, 0)),
                   pl.BlockSpec((tm, aw), lambda i: (i, 0)),
                   pl.BlockSpec((tm, aw), lambda i: (i, 0)),
                   pl.BlockSpec((tm, n - 3 * aw), lambda i: (i, 0))],
        out_shape=[jax.ShapeDtypeStruct((t, aw), BF16),
                   jax.ShapeDtypeStruct((t, aw), BF16),
                   jax.ShapeDtypeStruct((t, aw), BF16),
                   jax.ShapeDtypeStruct((t, n - 3 * aw), F32)],
        compiler_params=_cparams(("parallel",)),
        name="in_proj",
    )(x2, w_bf, *tables)


def _attn_kernel(q_ref, k_ref, v_ref, o_ref, vt_ref, km_ref, bias_ref, *, nb):
    blk = MOBA_BLOCK
    i = pl.program_id(2)
    lane = lax.broadcasted_iota(jnp.int32, (1, LANES), 1)
    head_lanes = (lane < HEAD_DIM, lane >= HEAD_DIM)

    @pl.when(i == 0)
    def _prepare():
        for j in range(nb):
            kb = k_ref[0, j * blk:(j + 1) * blk, :].astype(F32)
            km = jnp.sum(kb, axis=0, keepdims=True) * (1.0 / blk)
            km_ref[0, j:j + 1, :] = jnp.where(head_lanes[0], km, 0.0)
            km_ref[1, j:j + 1, :] = jnp.where(head_lanes[1], km, 0.0)
            vt = v_ref[0, j * blk:(j + 1) * blk, :].astype(F32).T
            vt_ref[0, 0:HEAD_DIM, j * blk:(j + 1) * blk] = vt[0:HEAD_DIM].astype(BF16)
            vt_ref[1, 0:HEAD_DIM, j * blk:(j + 1) * blk] = vt[HEAD_DIM:].astype(BF16)
        ones = jnp.ones((VT_ROWS - HEAD_DIM, nb * blk), BF16)
        vt_ref[0, HEAD_DIM:, :] = ones
        vt_ref[1, HEAD_DIM:, :] = ones

    q_pair = q_ref[0]
    nt = (((1,), (1,)), ((), ()))
    n_iota = lax.broadcasted_iota(jnp.int32, (nb, blk), 0)
    qh = []
    for h in range(2):
        qh.append(jnp.where(head_lanes[h], q_pair, jnp.zeros_like(q_pair)))
        g = lax.dot_general(km_ref[h], q_pair.astype(F32), nt, preferred_element_type=F32)
        valid = n_iota < i
        g = jnp.where(valid, g, -jnp.inf)
        rank = jnp.zeros((nb, blk), jnp.int32)
        for m in range(nb):
            gm = g[m:m + 1, :]
            ahead = (gm > g) | ((gm == g) & (m < n_iota))
            rank = rank + jnp.where(ahead, 1, 0)
        sel = (rank < MOBA_TOPK) & valid
        bias_ref[h] = jnp.where(sel, 0.0, NEG_BIG)

    off_i = pl.multiple_of(i * blk, blk)
    k_own = k_ref[0, pl.ds(off_i, blk), :]
    kpos = lax.broadcasted_iota(jnp.int32, (blk, blk), 0)
    qpos = lax.broadcasted_iota(jnp.int32, (blk, blk), 1)
    causal = kpos <= qpos
    carry = []
    for h in range(2):
        s = lax.dot_general(k_own, qh[h], nt, preferred_element_type=F32)
        s = jnp.where(causal, s, NEG_BIG)
        m0 = jnp.max(s, axis=0, keepdims=True)
        p = jnp.exp2(s - m0).astype(BF16)
        acc = jnp.dot(vt_ref[h, :, pl.ds(off_i, blk)], p, preferred_element_type=F32)
        carry += [m0, acc]

    def body(j, carry):
        off = pl.multiple_of(j * blk, blk)
        kb = k_ref[0, pl.ds(off, blk), :]
        new = []
        for h in range(2):
            m_run, acc = carry[2 * h], carry[2 * h + 1]
            s = lax.dot_general(kb, qh[h], nt, preferred_element_type=F32)
            bj = bias_ref[h, pl.ds(j, 1), :]
            mj = jnp.max(s, axis=0, keepdims=True) + bj
            m_new = jnp.maximum(m_run, mj)
            alpha = jnp.exp2(m_run - m_new)
            p = jnp.exp2(s - (m_new - bj)).astype(BF16)
            acc = alpha * acc + jnp.dot(vt_ref[h, :, pl.ds(off, blk)], p, preferred_element_type=F32)
            new += [m_new, acc]
        return tuple(new)

    carry = lax.fori_loop(0, i, body, tuple(carry))
    outs = []
    for h in range(2):
        acc = carry[2 * h + 1]
        outs.append(acc[0:HEAD_DIM] / acc[HEAD_DIM:HEAD_DIM + 1])
    o_ref[0] = jnp.concatenate(outs, axis=0).T


def _moba_attention(q, k, v):
    b, s, aw = q.shape
    nb = s // MOBA_BLOCK
    npair = aw // LANES
    return pl.pallas_call(
        functools.partial(_attn_kernel, nb=nb),
        grid=(b, npair, nb),
        in_specs=[pl.BlockSpec((1, MOBA_BLOCK, LANES), lambda b_, p, i: (b_, i, p)),
                  pl.BlockSpec((1, s, LANES), lambda b_, p, i: (b_, 0, p)),
                  pl.BlockSpec((1, s, LANES), lambda b_, p, i: (b_, 0, p))],
        out_specs=pl.BlockSpec((1, MOBA_BLOCK, LANES), lambda b_, p, i: (b_, i, p)),
        out_shape=jax.ShapeDtypeStruct((b, s, aw), F32),
        scratch_shapes=[pltpu.VMEM((2, VT_ROWS, s), BF16),
                        pltpu.VMEM((2, nb, LANES), F32),
                        pltpu.VMEM((2, nb, MOBA_BLOCK), F32)],
        compiler_params=_cparams(("parallel", "parallel", "arbitrary")),
        name="moba_attn",
    )(q, k, v)


def _rglru_kernel(rec_ref, cw_ref, cb_ref, wg_ref, ba_ref, bi_ref, lam_ref, o_ref, tail_ref, h_ref, *, rw):
    ts = rec_ref.shape[1]
    t_idx = pl.program_id(1)

    @pl.when(t_idx == 0)
    def _reset():
        tail_ref[...] = jnp.zeros_like(tail_ref)
        h_ref[...] = jnp.zeros_like(h_ref)

    xr = rec_ref[0, :, 0:rw]
    gr = rec_ref[0, :, rw:]
    row = lax.broadcasted_iota(jnp.int32, (ts, 1), 0)
    prev8 = tail_ref[...]
    xc = xr * cw_ref[CONV_WIDTH - 1:CONV_WIDTH, :] + cb_ref[...]
    for d in range(1, CONV_WIDTH):
        cur = pltpu.roll(xr, d, 0)
        head = jnp.tile(pltpu.roll(prev8, d, 0), (ts // 8, 1))
        xc = xc + jnp.where(row < d, head, cur) * cw_ref[CONV_WIDTH - 1 - d:CONV_WIDTH - d, :]
    tail_ref[...] = xr[ts - 8:, :]

    pre = jnp.dot(xc.astype(BF16), wg_ref[...], preferred_element_type=F32)
    r = jax.nn.sigmoid(pre[:, 0:rw] + ba_ref[...])
    gi = jax.nn.sigmoid(pre[:, rw:] + bi_ref[...])
    lam = lam_ref[...]
    softplus_neg = jnp.maximum(-lam, 0.0) + jnp.log1p(jnp.exp(-jnp.abs(lam)))
    log_a = (-LRU_C * r) * softplus_neg
    a = jnp.exp(log_a)
    u = jnp.sqrt(-jnp.tanh(log_a) * (a * a + 1.0)) * (gi * xc)

    d = 1
    while d < ts:
        a_sh = pltpu.roll(a, d, 0)
        u_sh = pltpu.roll(u, d, 0)
        live = row >= d
        u = jnp.where(live, a * u_sh + u, u)
        a = jnp.where(live, a * a_sh, a)
        d *= 2
    h = a * h_ref[...] + u
    h_ref[...] = h[ts - 1:, :]
    o_ref[0] = h * jax.nn.gelu(gr)


def _rglru(rec, conv_w, conv_b, w_gates_bf, b_a, b_i, lam):
    b, s, two_rw = rec.shape
    rw = two_rw // 2
    ts = min(512, s)
    full = lambda shape: pl.BlockSpec(shape, lambda b_, t: (0,) * len(shape))
    return pl.pallas_call(
        functools.partial(_rglru_kernel, rw=rw),
        grid=(b, s // ts),
        in_specs=[pl.BlockSpec((1, ts, two_rw), lambda b_, t: (b_, t, 0)),
                  full((CONV_WIDTH, rw)), full((1, rw)), full((rw, two_rw)),
                  full((1, rw)), full((1, rw)), full((1, rw))],
        out_specs=pl.BlockSpec((1, ts, rw), lambda b_, t: (b_, t, 0)),
        out_shape=jax.ShapeDtypeStruct((b, s, rw), F32),
        scratch_shapes=[pltpu.VMEM((8, rw), F32), pltpu.VMEM((1, rw), F32)],
        compiler_params=_cparams(("parallel", "arbitrary")),
        name="rglru",
    )(rec, conv_w, conv_b, w_gates_bf, b_a, b_i, lam)


def _block_diag(w):
    n, c, d = w.shape
    eye = jnp.eye(n, dtype=w.dtype)
    return (w[:, :, None, :] * eye[:, None, :, None]).reshape(n * c, n * d)


def _layer_norm(z, g, b):
    mu = jnp.mean(z, axis=-1, keepdims=True)
    zc = z - mu
    var = jnp.mean(zc * zc, axis=-1, keepdims=True)
    return zc * lax.rsqrt(var + LN_EPS) * g + b


def _rms(z, g):
    return z * lax.rsqrt(jnp.mean(z * z, axis=-1, keepdims=True) + RMS_EPS) * g


def _route(logits):
    lane = lax.broadcasted_iota(jnp.int32, logits.shape, 1)
    neg_inf = -jnp.inf
    gl = jnp.where(lane < N_GROUPS, logits, neg_inf)
    gmax = jnp.max(gl, axis=-1, keepdims=True)
    gsum = jnp.sum(jnp.exp(gl - gmax), axis=-1, keepdims=True)
    g_p = 1.0 / gsum
    g_idx = jnp.min(jnp.where(gl == gmax, lane, LANES), axis=-1, keepdims=True)
    lo = N_GROUPS + EXPERTS_PER_GROUP * g_idx
    emask = (lane >= lo) & (lane < lo + EXPERTS_PER_GROUP)
    el = jnp.where(emask, logits, neg_inf)
    emax = jnp.max(el, axis=-1, keepdims=True)
    ee = jnp.exp(el - emax)
    ep = ee / jnp.sum(ee, axis=-1, keepdims=True)
    e1 = jnp.max(ep, axis=-1, keepdims=True)
    i1 = jnp.min(jnp.where(emask & (ep == e1), lane, LANES), axis=-1, keepdims=True)
    rest = emask & (lane != i1)
    ep2 = jnp.where(rest, ep, neg_inf)
    e2 = jnp.max(ep2, axis=-1, keepdims=True)
    i2 = jnp.min(jnp.where(rest & (ep2 == e2), lane, LANES), axis=-1, keepdims=True)
    denom = e1 + e2
    w1 = g_p * e1 / denom
    w2 = g_p * e2 / denom
    return jnp.where(lane == i1, w1, 0.0) + jnp.where(lane == i2, w2, 0.0)


def _outproj_kernel(attn_ref, rec_ref, x_ref, ga_ref, gr_ref, w_ref, g_ref, b_ref, wr_ref, br_ref,
                    x1_ref, comb_ref, *, alpha):
    mixed = jnp.concatenate([_rms(attn_ref[...], ga_ref[...]), _rms(rec_ref[...], gr_ref[...])], axis=-1)
    h = jnp.dot(mixed.astype(BF16), w_ref[...], preferred_element_type=F32)
    x1 = _layer_norm(alpha * x_ref[...] + h, g_ref[...], b_ref[...])
    x1_ref[...] = x1
    logits = jnp.dot(x1.astype(BF16), wr_ref[...], preferred_element_type=F32) + br_ref[...]
    comb_ref[...] = _route(logits)


def _out_proj(attn2, rec2, x2, g_a, g_r, w_out_bf, ln_g, ln_b, w_router_bf, b_router, alpha):
    t, d = x2.shape
    aw, rw = attn2.shape[1], rec2.shape[1]
    tm = min(512, t)
    row = lambda w: pl.BlockSpec((tm, w), lambda i: (i, 0))
    full = lambda shape: pl.BlockSpec(shape, lambda i: (0,) * len(shape))
    return pl.pallas_call(
        functools.partial(_outproj_kernel, alpha=alpha),
        grid=(t // tm,),
        in_specs=[row(aw), row(rw), row(d), full((1, aw)), full((1, rw)), full((aw + rw, d)),
                  full((1, d)), full((1, d)), full((d, LANES)), full((1, LANES))],
        out_specs=[row(d), row(LANES)],
        out_shape=[jax.ShapeDtypeStruct((t, d), F32), jax.ShapeDtypeStruct((t, LANES), F32)],
        compiler_params=_cparams(("parallel",)),
        name="out_proj_ln_router",
    )(attn2, rec2, x2, g_a, g_r, w_out_bf, ln_g, ln_b, w_router_bf, b_router)


def _moe_kernel(x_ref, comb_ref, wg_ref, wu_ref, wd_ref, g_ref, b_ref, o_ref, acc_ref, *, alpha, n_experts):
    e = pl.program_id(1)

    @pl.when(e == 0)
    def _zero():
        acc_ref[...] = jnp.zeros_like(acc_ref)

    xb = x_ref[...].astype(BF16)
    gate = jnp.dot(xb, wg_ref[0], preferred_element_type=F32)
    up = jnp.dot(xb, wu_ref[0], preferred_element_type=F32)
    hidden = (jax.nn.silu(gate) * up).astype(BF16)
    y = jnp.dot(hidden, wd_ref[0], preferred_element_type=F32)
    comb = comb_ref[...]
    lane = lax.broadcasted_iota(jnp.int32, comb.shape, 1)
    w_e = jnp.sum(jnp.where(lane == e + N_GROUPS, comb, 0.0), axis=-1, keepdims=True)
    acc_ref[...] += w_e * y

    @pl.when(e == n_experts - 1)
    def _finish():
        o_ref[...] = _layer_norm(alpha * x_ref[...] + acc_ref[...], g_ref[...], b_ref[...])


def _moe(x1, comb, wg_bf, wu_bf, wd_bf, ln_g, ln_b, alpha):
    t, d = x1.shape
    n_experts, _, de = wg_bf.shape
    tm = min(1024, t)
    full = lambda shape: pl.BlockSpec(shape, lambda i, e: (0,) * len(shape))
    return pl.pallas_call(
        functools.partial(_moe_kernel, alpha=alpha, n_experts=n_experts),
        grid=(t // tm, n_experts),
        in_specs=[pl.BlockSpec((tm, d), lambda i, e: (i, 0)),
                  pl.BlockSpec((tm, LANES), lambda i, e: (i, 0)),
                  pl.BlockSpec((1, d, de), lambda i, e: (e, 0, 0)),
                  pl.BlockSpec((1, d, de), lambda i, e: (e, 0, 0)),
                  pl.BlockSpec((1, de, d), lambda i, e: (e, 0, 0)),
                  full((1, d)), full((1, d))],
        out_specs=pl.BlockSpec((tm, d), lambda i, e: (i, 0)),
        out_shape=jax.ShapeDtypeStruct((t, d), F32),
        scratch_shapes=[pltpu.VMEM((tm, d), F32)],
        compiler_params=_cparams(("parallel", "arbitrary")),
        name="moe_ln",
    )(x1, comb, wg_bf, wu_bf, wd_bf, ln_g, ln_b)


def kernel(x, w_in, conv_w, conv_b, w_rg_a, b_rg_a, w_rg_i, b_rg_i, lru_lambda, g_attn_norm, g_rec_norm, w_out,
           ln1_g, ln1_b, w_router_group, b_router_group, w_router_expert, b_router_expert, w_gate, w_up, w_down,
           ln2_g, ln2_b):
    b, s, d = x.shape
    depth = w_in.shape[0]
    rw = conv_w.shape[2]
    aw = (w_in.shape[2] - 2 * rw) // 3
    alpha = (2 * depth) ** 0.25
    tables = _rope_tables(s)
    row = lambda v: v.reshape(1, -1)

    x2 = x.reshape(b * s, d)
    for l in range(depth):
        q, k, v, rec = _in_proj(x2, w_in[l].astype(BF16), tables, s, aw)
        attn = _moba_attention(q.reshape(b, s, aw), k.reshape(b, s, aw), v.reshape(b, s, aw))
        w_gates = jnp.concatenate([_block_diag(w_rg_a[l]), _block_diag(w_rg_i[l])], axis=1).astype(BF16)
        rec_out = _rglru(rec.reshape(b, s, 2 * rw), conv_w[l], row(conv_b[l]), w_gates,
                         row(b_rg_a[l]), row(b_rg_i[l]), row(lru_lambda[l]))
        n_route = N_GROUPS + N_GROUPS * EXPERTS_PER_GROUP
        w_router = jnp.concatenate([w_router_group[l], w_router_expert[l]], axis=1)
        w_router = jnp.pad(w_router, ((0, 0), (0, LANES - n_route))).astype(BF16)
        b_router = jnp.pad(jnp.concatenate([b_router_group[l], b_router_expert[l]]), (0, LANES - n_route))
        x1, comb = _out_proj(attn.reshape(b * s, aw), rec_out.reshape(b * s, rw), x2,
                             row(g_attn_norm[l]), row(g_rec_norm[l]), w_out[l].astype(BF16),
                             row(ln1_g[l]), row(ln1_b[l]), w_router, row(b_router), alpha)
        x2 = _moe(x1, comb, w_gate[l].astype(BF16), w_up[l].astype(BF16), w_down[l].astype(BF16),
                  row(ln2_g[l]), row(ln2_b[l]), alpha)
    return x2.reshape(b, s, d)
```

```python
import functools
import math

import jax
import jax.numpy as jnp
from jax import lax
from jax.experimental import pallas as pl
from jax.experimental.pallas import tpu as pltpu

F32 = jnp.float32
BF16 = jnp.bfloat16

HEAD_DIM = 64
ROT_DIM = HEAD_DIM // 4
ROPE_THETA = 500000.0
MOBA_BLOCK = 256
MOBA_TOPK = 3
CONV_WIDTH = 4
LRU_C = 8.0
N_GROUPS = 4
EXPERTS_PER_GROUP = 4
LN_EPS = 1e-5
RMS_EPS = 1e-6

LANES = 128
NEG_BIG = -1e30
LOG2E = 1.4426950408889634
Q_SCALE = HEAD_DIM ** -0.5 * LOG2E
VT_ROWS = HEAD_DIM + 16
VMEM_LIMIT = 56 * 1024 * 1024


def _cparams(sem):
    return pltpu.CompilerParams(dimension_semantics=sem, vmem_limit_bytes=VMEM_LIMIT)


def _inproj_kernel(x_ref, w_ref, c_ref, s1_ref, s2_ref, q_ref, k_ref, v_ref, rec_ref, *, aw):
    xb = x_ref[...].astype(BF16)
    cos, sin_lo, sin_hi = c_ref[...], s1_ref[...], s2_ref[...]

    def rope(t):
        outs = []
        for c in range(aw // LANES):
            tc = t[:, c * LANES:(c + 1) * LANES]
            outs.append(tc * cos + pltpu.roll(tc, 8, 1) * sin_lo + pltpu.roll(tc, LANES - 8, 1) * sin_hi)
        return jnp.concatenate(outs, axis=1)

    q = jnp.dot(xb, w_ref[:, 0:aw], preferred_element_type=F32)
    q_ref[...] = (rope(q) * Q_SCALE).astype(BF16)
    k = jnp.dot(xb, w_ref[:, aw:2 * aw], preferred_element_type=F32)
    k_ref[...] = rope(k).astype(BF16)
    v = jnp.dot(xb, w_ref[:, 2 * aw:3 * aw], preferred_element_type=F32)
    v_ref[...] = v.astype(BF16)
    rec_ref[...] = jnp.dot(xb, w_ref[:, 3 * aw:], preferred_element_type=F32)


def _rope_tables(seq):
    half = ROT_DIM // 2
    inv_freq = ROPE_THETA ** (-jnp.arange(half, dtype=F32) * 2.0 / ROT_DIM)
    ang = jnp.arange(seq).astype(F32)[:, None] * inv_freq[None, :]
    cos, sin = jnp.cos(ang), jnp.sin(ang)
    zeros = jnp.zeros((seq, HEAD_DIM - ROT_DIM), F32)
    zh = jnp.zeros((seq, half), F32)
    c64 = jnp.concatenate([cos, cos, zeros + 1.0], axis=1)
    lo64 = jnp.concatenate([zh, sin, zeros], axis=1)
    hi64 = jnp.concatenate([-sin, zh, zeros], axis=1)
    rep = LANES // HEAD_DIM
    return jnp.tile(c64, (1, rep)), jnp.tile(lo64, (1, rep)), jnp.tile(hi64, (1, rep))


def _in_proj(x2, w_bf, tables, seq, aw):
    t, d = x2.shape
    n = w_bf.shape[1]
    tm = min(512, seq)
    nseq = seq // tm
    tab_spec = pl.BlockSpec((tm, LANES), lambda i: (i % nseq, 0))
    return pl.pallas_call(
        functools.partial(_inproj_kernel, aw=aw),
        grid=(t // tm,),
        in_specs=[pl.BlockSpec((tm, d), lambda i: (i, 0)),
                  pl.BlockSpec((d, n), lambda i: (0, 0)),
                  tab_spec, tab_spec, tab_spec],
        out_specs=[pl.BlockSpec((tm, aw), lambda i: (i, 0)),
                   pl.BlockSpec((tm, aw), lambda i: (i, 0)),
                   pl.BlockSpec((tm, aw), lambda i: (i, 0)),
                   pl.BlockSpec((tm, n - 3 * aw), lambda i: (i, 0))],
        out_shape=[jax.ShapeDtypeStruct((t, aw), BF16),
                   jax.ShapeDtypeStruct((t, aw), BF16),
                   jax.ShapeDtypeStruct((t, aw), BF16),
                   jax.ShapeDtypeStruct((t, n - 3 * aw), F32)],
        compiler_params=_cparams(("parallel",)),
        name="in_proj",
    )(x2, w_bf, *tables)


def _attn_kernel(q_ref, k_ref, v_ref, o_ref, vt_ref, km_ref, bias_ref, m_ref, acc_ref, *, nb, npair):
    blk = MOBA_BLOCK
    i = pl.program_id(1)
    lane = lax.broadcasted_iota(jnp.int32, (1, LANES), 1)
    head_lanes = (lane < HEAD_DIM, lane >= HEAD_DIM)
    nt = (((1,), (1,)), ((), ()))
    heads = [(p, h) for p in range(npair) for h in range(2)]
    pair_lanes = lambda p: slice(p * LANES, (p + 1) * LANES)

    @pl.when(i == 0)
    def _prepare():
        def prep(j, _):
            off = pl.multiple_of(j * blk, blk)
            for p in range(npair):
                kb = k_ref[0, pl.ds(off, blk), pair_lanes(p)].astype(F32)
                km = jnp.sum(kb, axis=0, keepdims=True) * (1.0 / blk)
                km_ref[2 * p, pl.ds(j, 1), :] = jnp.where(head_lanes[0], km, 0.0)
                km_ref[2 * p + 1, pl.ds(j, 1), :] = jnp.where(head_lanes[1], km, 0.0)
                vt = v_ref[0, pl.ds(off, blk), pair_lanes(p)].astype(F32).T
                vt_ref[2 * p, 0:HEAD_DIM, pl.ds(off, blk)] = vt[0:HEAD_DIM].astype(BF16)
                vt_ref[2 * p + 1, 0:HEAD_DIM, pl.ds(off, blk)] = vt[HEAD_DIM:].astype(BF16)
            return 0

        lax.fori_loop(0, nb, prep, 0)
        vt_ref[:, HEAD_DIM:, :] = jnp.ones((2 * npair, VT_ROWS - HEAD_DIM, nb * blk), BF16)

    n_iota = lax.broadcasted_iota(jnp.int32, (nb, blk), 0)
    valid = n_iota < i
    qh = []
    for p, h in heads:
        q_pair = q_ref[0, :, pair_lanes(p)]
        qh.append(jnp.where(head_lanes[h], q_pair, jnp.zeros_like(q_pair)))
        g = lax.dot_general(km_ref[2 * p + h], q_pair.astype(F32), nt, preferred_element_type=F32)
        g = jnp.where(valid, g, -jnp.inf)
        bias = jnp.full((nb, blk), NEG_BIG, F32)
        for _ in range(min(MOBA_TOPK, nb)):
            top = jnp.max(g, axis=0, keepdims=True)
            first = jnp.min(jnp.where(g == top, n_iota, nb), axis=0, keepdims=True)
            hit = n_iota == first
            bias = jnp.where(hit & valid, 0.0, bias)
            g = jnp.where(hit, -jnp.inf, g)
        bias_ref[2 * p + h] = bias

    off_i = pl.multiple_of(i * blk, blk)
    kpos = lax.broadcasted_iota(jnp.int32, (blk, blk), 0)
    qpos = lax.broadcasted_iota(jnp.int32, (blk, blk), 1)
    causal = kpos <= qpos
    def scores(n, off, rows):
        kb = k_ref[0, pl.ds(off, rows), pair_lanes(heads[n][0])]
        return lax.dot_general(kb, qh[n], nt, preferred_element_type=F32)

    def interleave(stage_scores, stage_probs, stage_values):
        nh = len(heads)
        s_q = {n: stage_scores(n) for n in range(min(2, nh))}
        p_q = {}
        for t in range(nh + 1):
            if t + 2 < nh:
                s_q[t + 2] = stage_scores(t + 2)
            if t < nh:
                p_q[t] = stage_probs(t, s_q.pop(t))
            if t >= 1:
                stage_values(t - 1, *p_q.pop(t - 1))

    def own_probs(n, s):
        s = jnp.where(causal, s, NEG_BIG)
        m0 = jnp.max(s, axis=0, keepdims=True)
        m_ref[n] = m0
        return (jnp.exp2(s - m0).astype(BF16),)

    def own_values(n, pr):
        acc_ref[n] = jnp.dot(vt_ref[n, :, pl.ds(off_i, blk)], pr, preferred_element_type=F32)

    interleave(lambda n: scores(n, off_i, blk), own_probs, own_values)

    def body(jj, _):
        j0 = 2 * jj
        off = pl.multiple_of(j0 * blk, 2 * blk)

        def probs(n, s):
            b0 = bias_ref[n, pl.ds(j0, 1), :]
            b1 = bias_ref[n, pl.ds(j0 + 1, 1), :]
            s0, s1 = s[0:blk], s[blk:]
            mj = jnp.maximum(jnp.max(s0, axis=0, keepdims=True) + b0, jnp.max(s1, axis=0, keepdims=True) + b1)
            m_run = m_ref[n]
            m_new = jnp.maximum(m_run, mj)
            m_ref[n] = m_new
            alpha = jnp.exp2(m_run - m_new)
            p0 = jnp.exp2(s0 - (m_new - b0)).astype(BF16)
            p1 = jnp.exp2(s1 - (m_new - b1)).astype(BF16)
            return alpha, p0, p1

        def accumulate(n, alpha, p0, p1):
            pv = (jnp.dot(vt_ref[n, :, pl.ds(off, blk)], p0, preferred_element_type=F32)
                  + jnp.dot(vt_ref[n, :, pl.ds(off + blk, blk)], p1, preferred_element_type=F32))
            acc_ref[n] = alpha * acc_ref[n] + pv

        interleave(lambda n: scores(n, off, 2 * blk), probs, accumulate)
        return 0

    lax.fori_loop(0, (i + 1) // 2, body, 0)
    for p in range(npair):
        outs = []
        for h in range(2):
            acc = acc_ref[2 * p + h]
            outs.append(acc[0:HEAD_DIM] / acc[HEAD_DIM:HEAD_DIM + 1])
        o_ref[0, :, pair_lanes(p)] = jnp.concatenate(outs, axis=0).T


def _moba_attention(q, k, v):
    b, s, aw = q.shape
    nb = s // MOBA_BLOCK
    npair = aw // LANES
    nh = 2 * npair
    return pl.pallas_call(
        functools.partial(_attn_kernel, nb=nb, npair=npair),
        grid=(b, nb),
        in_specs=[pl.BlockSpec((1, MOBA_BLOCK, aw), lambda b_, i: (b_, i, 0)),
                  pl.BlockSpec((1, s, aw), lambda b_, i: (b_, 0, 0)),
                  pl.BlockSpec((1, s, aw), lambda b_, i: (b_, 0, 0))],
        out_specs=pl.BlockSpec((1, MOBA_BLOCK, aw), lambda b_, i: (b_, i, 0)),
        out_shape=jax.ShapeDtypeStruct((b, s, aw), F32),
        scratch_shapes=[pltpu.VMEM((nh, VT_ROWS, s), BF16),
                        pltpu.VMEM((nh, nb, LANES), F32),
                        pltpu.VMEM((nh, nb, MOBA_BLOCK), F32),
                        pltpu.VMEM((nh, 1, MOBA_BLOCK), F32),
                        pltpu.VMEM((nh, VT_ROWS, MOBA_BLOCK), F32)],
        compiler_params=_cparams(("parallel", "arbitrary")),
        name="moba_attn",
    )(q, k, v)


def _rglru_kernel(rec_ref, cw_ref, cb_ref, wg_ref, ba_ref, bi_ref, lam_ref, o_ref, tail_ref, h_ref, *, rw):
    ts = rec_ref.shape[1]
    t_idx = pl.program_id(1)

    @pl.when(t_idx == 0)
    def _reset():
        tail_ref[...] = jnp.zeros_like(tail_ref)
        h_ref[...] = jnp.zeros_like(h_ref)

    xr = rec_ref[0, :, 0:rw]
    gr = rec_ref[0, :, rw:]
    row = lax.broadcasted_iota(jnp.int32, (ts, 1), 0)
    prev8 = tail_ref[...]
    xc = xr * cw_ref[CONV_WIDTH - 1:CONV_WIDTH, :] + cb_ref[...]
    for d in range(1, CONV_WIDTH):
        cur = pltpu.roll(xr, d, 0)
        head = jnp.tile(pltpu.roll(prev8, d, 0), (ts // 8, 1))
        xc = xc + jnp.where(row < d, head, cur) * cw_ref[CONV_WIDTH - 1 - d:CONV_WIDTH - d, :]
    tail_ref[...] = xr[ts - 8:, :]

    pre = jnp.dot(xc.astype(BF16), wg_ref[...], preferred_element_type=F32)
    r = jax.nn.sigmoid(pre[:, 0:rw] + ba_ref[...])
    gi = jax.nn.sigmoid(pre[:, rw:] + bi_ref[...])
    lam = lam_ref[...]
    softplus_neg = jnp.maximum(-lam, 0.0) + jnp.log1p(jnp.exp(-jnp.abs(lam)))
    log_a = (-LRU_C * r) * softplus_neg
    a = jnp.exp(log_a)
    u = jnp.sqrt(-jnp.tanh(log_a) * (a * a + 1.0)) * (gi * xc)

    d = 1
    while d < ts:
        a_sh = pltpu.roll(a, d, 0)
        u_sh = pltpu.roll(u, d, 0)
        live = row >= d
        u = jnp.where(live, a * u_sh + u, u)
        a = jnp.where(live, a * a_sh, a)
        d *= 2
    h = a * h_ref[...] + u
    h_ref[...] = h[ts - 1:, :]
    o_ref[0] = h * jax.nn.gelu(gr)


def _rglru(rec, conv_w, conv_b, w_gates_bf, b_a, b_i, lam):
    b, s, two_rw = rec.shape
    rw = two_rw // 2
    ts = min(512, s)
    full = lambda shape: pl.BlockSpec(shape, lambda b_, t: (0,) * len(shape))
    return pl.pallas_call(
        functools.partial(_rglru_kernel, rw=rw),
        grid=(b, s // ts),
        in_specs=[pl.BlockSpec((1, ts, two_rw), lambda b_, t: (b_, t, 0)),
                  full((CONV_WIDTH, rw)), full((1, rw)), full((rw, two_rw)),
                  full((1, rw)), full((1, rw)), full((1, rw))],
        out_specs=pl.BlockSpec((1, ts, rw), lambda b_, t: (b_, t, 0)),
        out_shape=jax.ShapeDtypeStruct((b, s, rw), F32),
        scratch_shapes=[pltpu.VMEM((8, rw), F32), pltpu.VMEM((1, rw), F32)],
        compiler_params=_cparams(("parallel", "arbitrary")),
        name="rglru",
    )(rec, conv_w, conv_b, w_gates_bf, b_a, b_i, lam)


def _block_diag(w):
    n, c, d = w.shape
    eye = jnp.eye(n, dtype=w.dtype)
    return (w[:, :, None, :] * eye[:, None, :, None]).reshape(n * c, n * d)


def _layer_norm(z, g, b):
    mu = jnp.mean(z, axis=-1, keepdims=True)
    zc = z - mu
    var = jnp.mean(zc * zc, axis=-1, keepdims=True)
    return zc * lax.rsqrt(var + LN_EPS) * g + b


def _rms(z, g):
    return z * lax.rsqrt(jnp.mean(z * z, axis=-1, keepdims=True) + RMS_EPS) * g


def _route(logits):
    lane = lax.broadcasted_iota(jnp.int32, logits.shape, 1)
    neg_inf = -jnp.inf
    gl = jnp.where(lane < N_GROUPS, logits, neg_inf)
    gmax = jnp.max(gl, axis=-1, keepdims=True)
    gsum = jnp.sum(jnp.exp(gl - gmax), axis=-1, keepdims=True)
    g_p = 1.0 / gsum
    g_idx = jnp.min(jnp.where(gl == gmax, lane, LANES), axis=-1, keepdims=True)
    lo = N_GROUPS + EXPERTS_PER_GROUP * g_idx
    emask = (lane >= lo) & (lane < lo + EXPERTS_PER_GROUP)
    el = jnp.where(emask, logits, neg_inf)
    emax = jnp.max(el, axis=-1, keepdims=True)
    ee = jnp.exp(el - emax)
    ep = ee / jnp.sum(ee, axis=-1, keepdims=True)
    e1 = jnp.max(ep, axis=-1, keepdims=True)
    i1 = jnp.min(jnp.where(emask & (ep == e1), lane, LANES), axis=-1, keepdims=True)
    rest = emask & (lane != i1)
    ep2 = jnp.where(rest, ep, neg_inf)
    e2 = jnp.max(ep2, axis=-1, keepdims=True)
    i2 = jnp.min(jnp.where(rest & (ep2 == e2), lane, LANES), axis=-1, keepdims=True)
    denom = e1 + e2
    w1 = g_p * e1 / denom
    w2 = g_p * e2 / denom
    return jnp.where(lane == i1, w1, 0.0) + jnp.where(lane == i2, w2, 0.0)


def _outproj_kernel(attn_ref, rec_ref, x_ref, ga_ref, gr_ref, w_ref, g_ref, b_ref, wr_ref, br_ref,
                    x1_ref, comb_ref, *, alpha):
    mixed = jnp.concatenate([_rms(attn_ref[...], ga_ref[...]), _rms(rec_ref[...], gr_ref[...])], axis=-1)
    h = jnp.dot(mixed.astype(BF16), w_ref[...], preferred_element_type=F32)
    x1 = _layer_norm(alpha * x_ref[...] + h, g_ref[...], b_ref[...])
    x1_ref[...] = x1
    logits = jnp.dot(x1.astype(BF16), wr_ref[...], preferred_element_type=F32) + br_ref[...]
    comb_ref[...] = _route(logits)


def _out_proj(attn2, rec2, x2, g_a, g_r, w_out_bf, ln_g, ln_b, w_router_bf, b_router, alpha):
    t, d = x2.shape
    aw, rw = attn2.shape[1], rec2.shape[1]
    tm = min(512, t)
    row = lambda w: pl.BlockSpec((tm, w), lambda i: (i, 0))
    full = lambda shape: pl.BlockSpec(shape, lambda i: (0,) * len(shape))
    return pl.pallas_call(
        functools.partial(_outproj_kernel, alpha=alpha),
        grid=(t // tm,),
        in_specs=[row(aw), row(rw), row(d), full((1, aw)), full((1, rw)), full((aw + rw, d)),
                  full((1, d)), full((1, d)), full((d, LANES)), full((1, LANES))],
        out_specs=[row(d), row(LANES)],
        out_shape=[jax.ShapeDtypeStruct((t, d), F32), jax.ShapeDtypeStruct((t, LANES), F32)],
        compiler_params=_cparams(("parallel",)),
        name="out_proj_ln_router",
    )(attn2, rec2, x2, g_a, g_r, w_out_bf, ln_g, ln_b, w_router_bf, b_router)


def _moe_kernel(x_ref, comb_ref, wg_ref, wu_ref, wd_ref, g_ref, b_ref, o_ref, acc_ref, *, alpha, n_experts):
    e = pl.program_id(1)

    @pl.when(e == 0)
    def _zero():
        acc_ref[...] = jnp.zeros_like(acc_ref)

    xb = x_ref[...].astype(BF16)
    gate = jnp.dot(xb, wg_ref[0], preferred_element_type=F32)
    up = jnp.dot(xb, wu_ref[0], preferred_element_type=F32)
    hidden = (jax.nn.silu(gate) * up).astype(BF16)
    y = jnp.dot(hidden, wd_ref[0], preferred_element_type=F32)
    comb = comb_ref[...]
    lane = lax.broadcasted_iota(jnp.int32, comb.shape, 1)
    w_e = jnp.sum(jnp.where(lane == e + N_GROUPS, comb, 0.0), axis=-1, keepdims=True)
    acc_ref[...] += w_e * y

    @pl.when(e == n_experts - 1)
    def _finish():
        o_ref[...] = _layer_norm(alpha * x_ref[...] + acc_ref[...], g_ref[...], b_ref[...])


def _moe(x1, comb, wg_bf, wu_bf, wd_bf, ln_g, ln_b, alpha):
    t, d = x1.shape
    n_experts, _, de = wg_bf.shape
    tm = min(1024, t)
    full = lambda shape: pl.BlockSpec(shape, lambda i, e: (0,) * len(shape))
    return pl.pallas_call(
        functools.partial(_moe_kernel, alpha=alpha, n_experts=n_experts),
        grid=(t // tm, n_experts),
        in_specs=[pl.BlockSpec((tm, d), lambda i, e: (i, 0)),
                  pl.BlockSpec((tm, LANES), lambda i, e: (i, 0)),
                  pl.BlockSpec((1, d, de), lambda i, e: (e, 0, 0)),
                  pl.BlockSpec((1, d, de), lambda i, e: (e, 0, 0)),
                  pl.BlockSpec((1, de, d), lambda i, e: (e, 0, 0)),
                  full((1, d)), full((1, d))],
        out_specs=pl.BlockSpec((tm, d), lambda i, e: (i, 0)),
        out_shape=jax.ShapeDtypeStruct((t, d), F32),
        scratch_shapes=[pltpu.VMEM((tm, d), F32)],
        compiler_params=_cparams(("parallel", "arbitrary")),
        name="moe_ln",
    )(x1, comb, wg_bf, wu_bf, wd_bf, ln_g, ln_b)


def kernel(x, w_in, conv_w, conv_b, w_rg_a, b_rg_a, w_rg_i, b_rg_i, lru_lambda, g_attn_norm, g_rec_norm, w_out,
           ln1_g, ln1_b, w_router_group, b_router_group, w_router_expert, b_router_expert, w_gate, w_up, w_down,
           ln2_g, ln2_b):
    b, s, d = x.shape
    depth = w_in.shape[0]
    rw = conv_w.shape[2]
    aw = (w_in.shape[2] - 2 * rw) // 3
    alpha = (2 * depth) ** 0.25
    tables = _rope_tables(s)
    row = lambda v: v.reshape(1, -1)

    x2 = x.reshape(b * s, d)
    for l in range(depth):
        q, k, v, rec = _in_proj(x2, w_in[l].astype(BF16), tables, s, aw)
        attn = _moba_attention(q.reshape(b, s, aw), k.reshape(b, s, aw), v.reshape(b, s, aw))
        w_gates = jnp.concatenate([_block_diag(w_rg_a[l]), _block_diag(w_rg_i[l])], axis=1).astype(BF16)
        rec_out = _rglru(rec.reshape(b, s, 2 * rw), conv_w[l], row(conv_b[l]), w_gates,
                         row(b_rg_a[l]), row(b_rg_i[l]), row(lru_lambda[l]))
        n_route = N_GROUPS + N_GROUPS * EXPERTS_PER_GROUP
        w_router = jnp.concatenate([w_router_group[l], w_router_expert[l]], axis=1)
        w_router = jnp.pad(w_router, ((0, 0), (0, LANES - n_route))).astype(BF16)
        b_router = jnp.pad(jnp.concatenate([b_router_group[l], b_router_expert[l]]), (0, LANES - n_route))
        x1, comb = _out_proj(attn.reshape(b * s, aw), rec_out.reshape(b * s, rw), x2,
                             row(g_attn_norm[l]), row(g_rec_norm[l]), w_out[l].astype(BF16),
                             row(ln1_g[l]), row(ln1_b[l]), w_router, row(b_router), alpha)
        x2 = _moe(x1, comb, w_gate[l].astype(BF16), w_up[l].astype(BF16), w_down[l].astype(BF16),
                  row(ln2_g[l]), row(ln2_b[l]), alpha)
    return x2.reshape(b, s, d)
```

```python
import functools
import math

import jax
import jax.numpy as jnp
from jax import lax
from jax.experimental import pallas as pl
from jax.experimental.pallas import tpu as pltpu

F32 = jnp.float32
BF16 = jnp.bfloat16

HEAD_DIM = 64
ROT_DIM = HEAD_DIM // 4
ROPE_THETA = 500000.0
MOBA_BLOCK = 256
MOBA_TOPK = 3
CONV_WIDTH = 4
LRU_C = 8.0
N_GROUPS = 4
EXPERTS_PER_GROUP = 4
LN_EPS = 1e-5
RMS_EPS = 1e-6

LANES = 128
NEG_BIG = -1e30
LOG2E = 1.4426950408889634
Q_SCALE = HEAD_DIM ** -0.5 * LOG2E
VT_ROWS = HEAD_DIM + 16
VMEM_LIMIT = 56 * 1024 * 1024


def _cparams(sem):
    return pltpu.CompilerParams(dimension_semantics=sem, vmem_limit_bytes=VMEM_LIMIT)


def _inproj_kernel(x_ref, w_ref, c_ref, s1_ref, s2_ref, q_ref, k_ref, v_ref, rec_ref, *, aw):
    xb = x_ref[...].astype(BF16)
    cos, sin_lo, sin_hi = c_ref[...], s1_ref[...], s2_ref[...]

    def rope(t):
        outs = []
        for c in range(aw // LANES):
            tc = t[:, c * LANES:(c + 1) * LANES]
            outs.append(tc * cos + pltpu.roll(tc, 8, 1) * sin_lo + pltpu.roll(tc, LANES - 8, 1) * sin_hi)
        return jnp.concatenate(outs, axis=1)

    q = jnp.dot(xb, w_ref[:, 0:aw], preferred_element_type=F32)
    q_ref[...] = (rope(q) * Q_SCALE).astype(BF16)
    k = jnp.dot(xb, w_ref[:, aw:2 * aw], preferred_element_type=F32)
    k_ref[...] = rope(k).astype(BF16)
    v = jnp.dot(xb, w_ref[:, 2 * aw:3 * aw], preferred_element_type=F32)
    v_ref[...] = v.astype(BF16)
    rec_ref[...] = jnp.dot(xb, w_ref[:, 3 * aw:], preferred_element_type=F32)


def _rope_tables(seq):
    half = ROT_DIM // 2
    inv_freq = ROPE_THETA ** (-jnp.arange(half, dtype=F32) * 2.0 / ROT_DIM)
    ang = jnp.arange(seq).astype(F32)[:, None] * inv_freq[None, :]
    cos, sin = jnp.cos(ang), jnp.sin(ang)
    zeros = jnp.zeros((seq, HEAD_DIM - ROT_DIM), F32)
    zh = jnp.zeros((seq, half), F32)
    c64 = jnp.concatenate([cos, cos, zeros + 1.0], axis=1)
    lo64 = jnp.concatenate([zh, sin, zeros], axis=1)
    hi64 = jnp.concatenate([-sin, zh, zeros], axis=1)
    rep = LANES // HEAD_DIM
    return jnp.tile(c64, (1, rep)), jnp.tile(lo64, (1, rep)), jnp.tile(hi64, (1, rep))


def _in_proj(x2, t, w_bf, tables, seq, aw):
    d = x2.shape[1]
    n = w_bf.shape[1]
    tm = min(512, seq)
    nseq = seq // tm
    tab_spec = pl.BlockSpec((tm, LANES), lambda i: (i % nseq, 0))
    return pl.pallas_call(
        functools.partial(_inproj_kernel, aw=aw),
        grid=(t // tm,),
        in_specs=[pl.BlockSpec((tm, d), lambda i: (i, 0)),
                  pl.BlockSpec((d, n), lambda i: (0, 0)),
                  tab_spec, tab_spec, tab_spec],
        out_specs=[pl.BlockSpec((tm, aw), lambda i: (i, 0)),
                   pl.BlockSpec((tm, aw), lambda i: (i, 0)),
                   pl.BlockSpec((tm, aw), lambda i: (i, 0)),
                   pl.BlockSpec((tm, n - 3 * aw), lambda i: (i, 0))],
        out_shape=[jax.ShapeDtypeStruct((t, aw), BF16),
                   jax.ShapeDtypeStruct((t, aw), BF16),
                   jax.ShapeDtypeStruct((t, aw), BF16),
                   jax.ShapeDtypeStruct((t, n - 3 * aw), F32)],
        compiler_params=_cparams(("parallel",)),
        name="in_proj",
    )(x2, w_bf, *tables)


def _attn_kernel(q_ref, k_ref, v_ref, o_ref, vt_ref, km_ref, bias_ref, m_ref, acc_ref, *, nb, npair):
    blk = MOBA_BLOCK
    i = pl.program_id(1)
    lane = lax.broadcasted_iota(jnp.int32, (1, LANES), 1)
    head_lanes = (lane < HEAD_DIM, lane >= HEAD_DIM)
    nt = (((1,), (1,)), ((), ()))
    heads = [(p, h) for p in range(npair) for h in range(2)]
    pair_lanes = lambda p: slice(p * LANES, (p + 1) * LANES)

    @pl.when(i == 0)
    def _prepare():
        def prep(j, _):
            off = pl.multiple_of(j * blk, blk)
            for p in range(npair):
                kb = k_ref[0, pl.ds(off, blk), pair_lanes(p)].astype(F32)
                km = jnp.sum(kb, axis=0, keepdims=True) * (1.0 / blk)
                km_ref[2 * p, pl.ds(j, 1), :] = jnp.where(head_lanes[0], km, 0.0)
                km_ref[2 * p + 1, pl.ds(j, 1), :] = jnp.where(head_lanes[1], km, 0.0)
                vt = v_ref[0, pl.ds(off, blk), pair_lanes(p)].astype(F32).T
                vt_ref[2 * p, 0:HEAD_DIM, pl.ds(off, blk)] = vt[0:HEAD_DIM].astype(BF16)
                vt_ref[2 * p + 1, 0:HEAD_DIM, pl.ds(off, blk)] = vt[HEAD_DIM:].astype(BF16)
            return 0

        lax.fori_loop(0, nb, prep, 0)
        vt_ref[:, HEAD_DIM:, :] = jnp.ones((2 * npair, VT_ROWS - HEAD_DIM, nb * blk), BF16)

    n_iota = lax.broadcasted_iota(jnp.int32, (nb, blk), 0)
    valid = n_iota < i
    qh = []
    for p, h in heads:
        q_pair = q_ref[0, :, pair_lanes(p)]
        qh.append(jnp.where(head_lanes[h], q_pair, jnp.zeros_like(q_pair)))
        g = lax.dot_general(km_ref[2 * p + h], q_pair.astype(F32), nt, preferred_element_type=F32)
        g = jnp.where(valid, g, -jnp.inf)
        bias = jnp.full((nb, blk), NEG_BIG, F32)
        for _ in range(min(MOBA_TOPK, nb)):
            top = jnp.max(g, axis=0, keepdims=True)
            first = jnp.min(jnp.where(g == top, n_iota, nb), axis=0, keepdims=True)
            hit = n_iota == first
            bias = jnp.where(hit & valid, 0.0, bias)
            g = jnp.where(hit, -jnp.inf, g)
        bias_ref[2 * p + h] = bias

    off_i = pl.multiple_of(i * blk, blk)
    kpos = lax.broadcasted_iota(jnp.int32, (blk, blk), 0)
    qpos = lax.broadcasted_iota(jnp.int32, (blk, blk), 1)
    causal = kpos <= qpos
    def scores(n, off, rows):
        kb = k_ref[0, pl.ds(off, rows), pair_lanes(heads[n][0])]
        return lax.dot_general(kb, qh[n], nt, preferred_element_type=F32)

    def interleave(stage_scores, stage_probs, stage_values):
        nh = len(heads)
        s_q = {n: stage_scores(n) for n in range(min(2, nh))}
        p_q = {}
        for t in range(nh + 1):
            if t + 2 < nh:
                s_q[t + 2] = stage_scores(t + 2)
            if t < nh:
                p_q[t] = stage_probs(t, s_q.pop(t))
            if t >= 1:
                stage_values(t - 1, *p_q.pop(t - 1))

    def own_probs(n, s):
        s = jnp.where(causal, s, NEG_BIG)
        m0 = jnp.max(s, axis=0, keepdims=True)
        m_ref[n] = m0
        return (jnp.exp2(s - m0).astype(BF16),)

    def own_values(n, pr):
        acc_ref[n] = jnp.dot(vt_ref[n, :, pl.ds(off_i, blk)], pr, preferred_element_type=F32)

    interleave(lambda n: scores(n, off_i, blk), own_probs, own_values)

    def body(jj, _):
        j0 = 2 * jj
        off = pl.multiple_of(j0 * blk, 2 * blk)

        def probs(n, s):
            b0 = bias_ref[n, pl.ds(j0, 1), :]
            b1 = bias_ref[n, pl.ds(j0 + 1, 1), :]
            s0, s1 = s[0:blk], s[blk:]
            mj = jnp.maximum(jnp.max(s0, axis=0, keepdims=True) + b0, jnp.max(s1, axis=0, keepdims=True) + b1)
            m_run = m_ref[n]
            m_new = jnp.maximum(m_run, mj)
            m_ref[n] = m_new
            alpha = jnp.exp2(m_run - m_new)
            p0 = jnp.exp2(s0 - (m_new - b0)).astype(BF16)
            p1 = jnp.exp2(s1 - (m_new - b1)).astype(BF16)
            return alpha, p0, p1

        def accumulate(n, alpha, p0, p1):
            pv = (jnp.dot(vt_ref[n, :, pl.ds(off, blk)], p0, preferred_element_type=F32)
                  + jnp.dot(vt_ref[n, :, pl.ds(off + blk, blk)], p1, preferred_element_type=F32))
            acc_ref[n] = alpha * acc_ref[n] + pv

        interleave(lambda n: scores(n, off, 2 * blk), probs, accumulate)
        return 0

    lax.fori_loop(0, (i + 1) // 2, body, 0)
    for p in range(npair):
        outs = []
        for h in range(2):
            acc = acc_ref[2 * p + h]
            outs.append(acc[0:HEAD_DIM] / acc[HEAD_DIM:HEAD_DIM + 1])
        o_ref[0, :, pair_lanes(p)] = jnp.concatenate(outs, axis=0).T


def _moba_attention(q, k, v):
    b, s, aw = q.shape
    nb = s // MOBA_BLOCK
    npair = aw // LANES
    nh = 2 * npair
    return pl.pallas_call(
        functools.partial(_attn_kernel, nb=nb, npair=npair),
        grid=(b, nb),
        in_specs=[pl.BlockSpec((1, MOBA_BLOCK, aw), lambda b_, i: (b_, i, 0)),
                  pl.BlockSpec((1, s, aw), lambda b_, i: (b_, 0, 0)),
                  pl.BlockSpec((1, s, aw), lambda b_, i: (b_, 0, 0))],
        out_specs=pl.BlockSpec((1, MOBA_BLOCK, aw), lambda b_, i: (b_, i, 0)),
        out_shape=jax.ShapeDtypeStruct((b, s, aw), F32),
        scratch_shapes=[pltpu.VMEM((nh, VT_ROWS, s), BF16),
                        pltpu.VMEM((nh, nb, LANES), F32),
                        pltpu.VMEM((nh, nb, MOBA_BLOCK), F32),
                        pltpu.VMEM((nh, 1, MOBA_BLOCK), F32),
                        pltpu.VMEM((nh, VT_ROWS, MOBA_BLOCK), F32)],
        compiler_params=_cparams(("parallel", "arbitrary")),
        name="moba_attn",
    )(q, k, v)


def _rglru_kernel(rec_ref, cw_ref, cb_ref, wg_ref, ba_ref, bi_ref, lam_ref, o_ref, tail_ref, h_ref, *, rw):
    ts = rec_ref.shape[1]
    t_idx = pl.program_id(1)

    @pl.when(t_idx == 0)
    def _reset():
        tail_ref[...] = jnp.zeros_like(tail_ref)
        h_ref[...] = jnp.zeros_like(h_ref)

    xr = rec_ref[0, :, 0:rw]
    gr = rec_ref[0, :, rw:]
    row = lax.broadcasted_iota(jnp.int32, (ts, 1), 0)
    prev8 = tail_ref[...]
    xc = xr * cw_ref[CONV_WIDTH - 1:CONV_WIDTH, :] + cb_ref[...]
    for d in range(1, CONV_WIDTH):
        cur = pltpu.roll(xr, d, 0)
        head = jnp.tile(pltpu.roll(prev8, d, 0), (ts // 8, 1))
        xc = xc + jnp.where(row < d, head, cur) * cw_ref[CONV_WIDTH - 1 - d:CONV_WIDTH - d, :]
    tail_ref[...] = xr[ts - 8:, :]

    pre = jnp.dot(xc.astype(BF16), wg_ref[...], preferred_element_type=F32)
    r = jax.nn.sigmoid(pre[:, 0:rw] + ba_ref[...])
    gi = jax.nn.sigmoid(pre[:, rw:] + bi_ref[...])
    lam = lam_ref[...]
    softplus_neg = jnp.maximum(-lam, 0.0) + jnp.log1p(jnp.exp(-jnp.abs(lam)))
    log_a = (-LRU_C * r) * softplus_neg
    a = jnp.exp(log_a)
    u = jnp.sqrt(-jnp.tanh(log_a) * (a * a + 1.0)) * (gi * xc)

    d = 1
    while d < ts:
        a_sh = pltpu.roll(a, d, 0)
        u_sh = pltpu.roll(u, d, 0)
        live = row >= d
        u = jnp.where(live, a * u_sh + u, u)
        a = jnp.where(live, a * a_sh, a)
        d *= 2
    h = a * h_ref[...] + u
    h_ref[...] = h[ts - 1:, :]
    o_ref[0] = h * jax.nn.gelu(gr)


def _rglru(rec, conv_w, conv_b, w_gates_bf, b_a, b_i, lam):
    b, s, two_rw = rec.shape
    rw = two_rw // 2
    ts = min(512, s)
    full = lambda shape: pl.BlockSpec(shape, lambda b_, t: (0,) * len(shape))
    return pl.pallas_call(
        functools.partial(_rglru_kernel, rw=rw),
        grid=(b, s // ts),
        in_specs=[pl.BlockSpec((1, ts, two_rw), lambda b_, t: (b_, t, 0)),
                  full((CONV_WIDTH, rw)), full((1, rw)), full((rw, two_rw)),
                  full((1, rw)), full((1, rw)), full((1, rw))],
        out_specs=pl.BlockSpec((1, ts, rw), lambda b_, t: (b_, t, 0)),
        out_shape=jax.ShapeDtypeStruct((b, s, rw), F32),
        scratch_shapes=[pltpu.VMEM((8, rw), F32), pltpu.VMEM((1, rw), F32)],
        compiler_params=_cparams(("parallel", "arbitrary")),
        name="rglru",
    )(rec, conv_w, conv_b, w_gates_bf, b_a, b_i, lam)


def _block_diag(w):
    n, c, d = w.shape
    eye = jnp.eye(n, dtype=w.dtype)
    return (w[:, :, None, :] * eye[:, None, :, None]).reshape(n * c, n * d)


def _layer_norm(z, g, b):
    mu = jnp.mean(z, axis=-1, keepdims=True)
    zc = z - mu
    var = jnp.mean(zc * zc, axis=-1, keepdims=True)
    return zc * lax.rsqrt(var + LN_EPS) * g + b


def _rms(z, g):
    return z * lax.rsqrt(jnp.mean(z * z, axis=-1, keepdims=True) + RMS_EPS) * g


def _route(logits):
    lane = lax.broadcasted_iota(jnp.int32, logits.shape, 1)
    neg_inf = -jnp.inf
    gl = jnp.where(lane < N_GROUPS, logits, neg_inf)
    gmax = jnp.max(gl, axis=-1, keepdims=True)
    gsum = jnp.sum(jnp.exp(gl - gmax), axis=-1, keepdims=True)
    g_p = 1.0 / gsum
    g_idx = jnp.min(jnp.where(gl == gmax, lane, LANES), axis=-1, keepdims=True)
    lo = N_GROUPS + EXPERTS_PER_GROUP * g_idx
    emask = (lane >= lo) & (lane < lo + EXPERTS_PER_GROUP)
    el = jnp.where(emask, logits, neg_inf)
    emax = jnp.max(el, axis=-1, keepdims=True)
    ee = jnp.exp(el - emax)
    ep = ee / jnp.sum(ee, axis=-1, keepdims=True)
    e1 = jnp.max(ep, axis=-1, keepdims=True)
    i1 = jnp.min(jnp.where(emask & (ep == e1), lane, LANES), axis=-1, keepdims=True)
    rest = emask & (lane != i1)
    ep2 = jnp.where(rest, ep, neg_inf)
    e2 = jnp.max(ep2, axis=-1, keepdims=True)
    i2 = jnp.min(jnp.where(rest & (ep2 == e2), lane, LANES), axis=-1, keepdims=True)
    denom = e1 + e2
    w1 = g_p * e1 / denom
    w2 = g_p * e2 / denom
    comb = jnp.where(lane == i1, w1, 0.0) + jnp.where(lane == i2, w2, 0.0)
    return jnp.where(lane == 0, g_idx.astype(F32), comb)


def _outproj_kernel(attn_ref, rec_ref, x_ref, ga_ref, gr_ref, w_ref, g_ref, b_ref, wr_ref, br_ref,
                    x1_ref, *, alpha, d):
    mixed = jnp.concatenate([_rms(attn_ref[...], ga_ref[...]), _rms(rec_ref[...], gr_ref[...])], axis=-1)
    h = jnp.dot(mixed.astype(BF16), w_ref[...], preferred_element_type=F32)
    x1 = _layer_norm(alpha * x_ref[...] + h, g_ref[...], b_ref[...])
    x1_ref[:, 0:d] = x1
    logits = jnp.dot(x1.astype(BF16), wr_ref[...], preferred_element_type=F32) + br_ref[...]
    x1_ref[:, d:] = _route(logits)


def _out_proj(attn2, rec2, x2, g_a, g_r, w_out_bf, ln_g, ln_b, w_router_bf, b_router, alpha):
    t, d = attn2.shape[0], x2.shape[1]
    aw, rw = attn2.shape[1], rec2.shape[1]
    tm = min(512, t)
    row = lambda w: pl.BlockSpec((tm, w), lambda i: (i, 0))
    full = lambda shape: pl.BlockSpec(shape, lambda i: (0,) * len(shape))
    return pl.pallas_call(
        functools.partial(_outproj_kernel, alpha=alpha, d=d),
        grid=(t // tm,),
        in_specs=[row(aw), row(rw), row(d), full((1, aw)), full((1, rw)), full((aw + rw, d)),
                  full((1, d)), full((1, d)), full((d, LANES)), full((1, LANES))],
        out_specs=row(d + LANES),
        out_shape=jax.ShapeDtypeStruct((t, d + LANES), F32),
        compiler_params=_cparams(("parallel",)),
        name="out_proj_ln_router",
    )(attn2, rec2, x2, g_a, g_r, w_out_bf, ln_g, ln_b, w_router_bf, b_router)


def _dispatch_plan(gidx, tm):
    t = gidx.shape[0]
    n_tiles = t // tm + N_GROUPS
    counts = jnp.sum((gidx[:, None] == jnp.arange(N_GROUPS)[None, :]).astype(jnp.int32), axis=0)
    tiles_g = (counts + tm - 1) // tm
    tile_end = jnp.cumsum(tiles_g)
    tile_start = tile_end - tiles_g
    n_used = tile_end[-1]
    sorted_start = jnp.cumsum(counts) - counts
    order = jnp.argsort(gidx, stable=True).astype(jnp.int32)
    tile = jnp.arange(n_tiles, dtype=jnp.int32)
    tg = jnp.sum((tile_end[None, :] <= tile[:, None]).astype(jnp.int32), axis=1)
    used = tile < n_used
    tg = jnp.where(used, tg, tg[n_used - 1])
    row_off = (tile - tile_start[tg]) * tm
    nvalid = jnp.where(used, jnp.clip(counts[tg] - row_off, 0, tm), 0).astype(jnp.int32)
    r = jnp.arange(tm, dtype=jnp.int32)[None, :]
    pos = sorted_start[tg][:, None] + row_off[:, None] + jnp.minimum(r, nvalid[:, None] - 1)
    src = order[jnp.clip(pos, 0, t - 1)]
    return src.reshape(-1), tg.astype(jnp.int32), nvalid


def _moe_kernel(src_ref, tg_ref, nv_ref, x_hbm, wg_ref, wu_ref, wd_ref, g_ref, b_ref, out_hbm,
                xbuf0, xbuf1, obuf0, obuf1, gsem, ssem, *, alpha, tm, d, n_tiles, t):
    i = pl.program_id(0)
    nxt = jnp.minimum(i + 1, n_tiles - 1)
    prv = jnp.maximum(i - 1, 0)
    nv_prv = jnp.where(i > 0, nv_ref[prv], 0)

    def gather_row(tile, r, xbuf, sem):
        tok = src_ref[tile * tm + r]
        return pltpu.make_async_copy(x_hbm.at[pl.ds(tok, 1), :], xbuf.at[pl.ds(r, 1), :], sem)

    def scatter_row(tile, n_valid, r, obuf, sem):
        dst = jnp.where(r < n_valid, src_ref[tile * tm + r], t + r)
        return pltpu.make_async_copy(obuf.at[pl.ds(r, 1), :], out_hbm.at[pl.ds(dst, 1), :], sem)

    gather_tile = lambda xbuf, sem: pltpu.make_async_copy(x_hbm.at[pl.ds(0, tm), :], xbuf, sem)
    scatter_tile = lambda obuf, sem: pltpu.make_async_copy(obuf, out_hbm.at[pl.ds(0, tm), :], sem)

    def for_rows(fn):
        def body(r, _):
            fn(r)
            return 0
        lax.fori_loop(0, tm, body, 0, unroll=8)

    def step(x_cur, x_nxt, o_cur, o_prv, g_cur, g_nxt, s_cur, s_prv):
        @pl.when(i == 0)
        def _prologue():
            for_rows(lambda r: gather_row(0, r, x_cur, g_cur).start())
            o_prv[...] = jnp.zeros_like(o_prv)

        gather_tile(x_cur, g_cur).wait()
        x = x_cur[:, 0:d]
        route = x_cur[:, d:]
        xb = x.astype(BF16)
        lane = lax.broadcasted_iota(jnp.int32, route.shape, 1)
        base = N_GROUPS + EXPERTS_PER_GROUP * tg_ref[i]
        rows_per = tm // EXPERTS_PER_GROUP
        y = jnp.zeros((tm, d), F32)
        for e in range(EXPERTS_PER_GROUP):
            for r in range(e * rows_per, (e + 1) * rows_per):
                gather_row(nxt, r, x_nxt, g_nxt).start()
                scatter_row(prv, nv_prv, r, o_prv, s_prv).start()
            gate = jnp.dot(xb, wg_ref[e], preferred_element_type=F32)
            up = jnp.dot(xb, wu_ref[e], preferred_element_type=F32)
            hidden = (jax.nn.silu(gate) * up).astype(BF16)
            w_e = jnp.sum(jnp.where(lane == base + e, route, 0.0), axis=-1, keepdims=True)
            y = y + w_e * jnp.dot(hidden, wd_ref[e], preferred_element_type=F32)
        o_cur[...] = _layer_norm(alpha * x + y, g_ref[...], b_ref[...])
        scatter_tile(o_prv, s_prv).wait()

        @pl.when(i == n_tiles - 1)
        def _epilogue():
            gather_tile(x_nxt, g_nxt).wait()
            for_rows(lambda r: scatter_row(i, nv_ref[i], r, o_cur, s_cur).start())
            scatter_tile(o_cur, s_cur).wait()

    @pl.when(i % 2 == 0)
    def _even():
        step(xbuf0, xbuf1, obuf0, obuf1, gsem.at[0], gsem.at[1], ssem.at[0], ssem.at[1])

    @pl.when(i % 2 == 1)
    def _odd():
        step(xbuf1, xbuf0, obuf1, obuf0, gsem.at[1], gsem.at[0], ssem.at[1], ssem.at[0])


def _moe(x1e, t, wg_bf, wu_bf, wd_bf, ln_g, ln_b, alpha, d):
    n_experts, _, de = wg_bf.shape
    tm = min(512, t)
    gidx = x1e[:, d].astype(jnp.int32)
    src, tg, nvalid = _dispatch_plan(gidx, tm)
    n_tiles = tg.shape[0]
    epg = EXPERTS_PER_GROUP
    full = lambda shape: pl.BlockSpec(shape, lambda i, *_: (0,) * len(shape))
    grid_spec = pltpu.PrefetchScalarGridSpec(
        num_scalar_prefetch=3,
        grid=(n_tiles,),
        in_specs=[pl.BlockSpec(memory_space=pl.ANY),
                  pl.BlockSpec((epg, d, de), lambda i, src_, tg_, nv_: (tg_[i], 0, 0)),
                  pl.BlockSpec((epg, d, de), lambda i, src_, tg_, nv_: (tg_[i], 0, 0)),
                  pl.BlockSpec((epg, de, d), lambda i, src_, tg_, nv_: (tg_[i], 0, 0)),
                  full((1, d)), full((1, d))],
        out_specs=pl.BlockSpec(memory_space=pl.ANY),
        scratch_shapes=[pltpu.VMEM((tm, d + LANES), F32), pltpu.VMEM((tm, d + LANES), F32),
                        pltpu.VMEM((tm, d), F32), pltpu.VMEM((tm, d), F32),
                        pltpu.SemaphoreType.DMA((2,)),
                        pltpu.SemaphoreType.DMA((2,))],
    )
    return pl.pallas_call(
        functools.partial(_moe_kernel, alpha=alpha, tm=tm, d=d, n_tiles=n_tiles, t=t),
        grid_spec=grid_spec,
        out_shape=jax.ShapeDtypeStruct((t + tm, d), F32),
        compiler_params=_cparams(("arbitrary",)),
        name="moe_ln",
    )(src, tg, nvalid, x1e, wg_bf, wu_bf, wd_bf, ln_g, ln_b)


def kernel(x, w_in, conv_w, conv_b, w_rg_a, b_rg_a, w_rg_i, b_rg_i, lru_lambda, g_attn_norm, g_rec_norm, w_out,
           ln1_g, ln1_b, w_router_group, b_router_group, w_router_expert, b_router_expert, w_gate, w_up, w_down,
           ln2_g, ln2_b):
    b, s, d = x.shape
    depth = w_in.shape[0]
    rw = conv_w.shape[2]
    aw = (w_in.shape[2] - 2 * rw) // 3
    alpha = (2 * depth) ** 0.25
    tables = _rope_tables(s)
    row = lambda v: v.reshape(1, -1)

    x2 = x.reshape(b * s, d)
    for l in range(depth):
        q, k, v, rec = _in_proj(x2, b * s, w_in[l].astype(BF16), tables, s, aw)
        attn = _moba_attention(q.reshape(b, s, aw), k.reshape(b, s, aw), v.reshape(b, s, aw))
        w_gates = jnp.concatenate([_block_diag(w_rg_a[l]), _block_diag(w_rg_i[l])], axis=1).astype(BF16)
        rec_out = _rglru(rec.reshape(b, s, 2 * rw), conv_w[l], row(conv_b[l]), w_gates,
                         row(b_rg_a[l]), row(b_rg_i[l]), row(lru_lambda[l]))
        n_route = N_GROUPS + N_GROUPS * EXPERTS_PER_GROUP
        w_router = jnp.concatenate([w_router_group[l], w_router_expert[l]], axis=1)
        w_router = jnp.pad(w_router, ((0, 0), (0, LANES - n_route))).astype(BF16)
        b_router = jnp.pad(jnp.concatenate([b_router_group[l], b_router_expert[l]]), (0, LANES - n_route))
        x1e = _out_proj(attn.reshape(b * s, aw), rec_out.reshape(b * s, rw), x2,
                        row(g_attn_norm[l]), row(g_rec_norm[l]), w_out[l].astype(BF16),
                        row(ln1_g[l]), row(ln1_b[l]), w_router, row(b_router), alpha)
        x2 = _moe(x1e, b * s, w_gate[l].astype(BF16), w_up[l].astype(BF16), w_down[l].astype(BF16),
                  row(ln2_g[l]), row(ln2_b[l]), alpha, d)
    return x2[:b * s].reshape(b, s, d)
```

```python
import functools

import jax
import jax.numpy as jnp
from jax import lax
from jax.experimental import pallas as pl
from jax.experimental.pallas import tpu as pltpu

F32 = jnp.float32
BF16 = jnp.bfloat16

HEAD_DIM = 64
ROT_DIM = HEAD_DIM // 4
ROPE_THETA = 500000.0
MOBA_BLOCK = 256
MOBA_TOPK = 3
CONV_WIDTH = 4
LRU_C = 8.0
N_GROUPS = 4
EXPERTS_PER_GROUP = 4
LN_EPS = 1e-5
RMS_EPS = 1e-6

LANES = 128
SUBLANES = 8
NEG_BIG = -1e30
LOG2E = 1.4426950408889634
Q_SCALE = HEAD_DIM ** -0.5 * LOG2E
AHEAD, BEHIND = 3, 1
VT_ROWS = HEAD_DIM + 16
VMEM_LIMIT = 56 * 1024 * 1024


def _cparams(sem):
    return pltpu.CompilerParams(dimension_semantics=sem, vmem_limit_bytes=VMEM_LIMIT)


def _inproj_kernel(x_ref, w_ref, c_ref, s1_ref, s2_ref, q_ref, k_ref, v_ref, rec_ref, *, aw):
    xb = x_ref[...].astype(BF16)
    cos, sin_lo, sin_hi = c_ref[...], s1_ref[...], s2_ref[...]

    def rope(t):
        outs = []
        for c in range(aw // LANES):
            tc = t[:, c * LANES:(c + 1) * LANES]
            outs.append(tc * cos + pltpu.roll(tc, 8, 1) * sin_lo + pltpu.roll(tc, LANES - 8, 1) * sin_hi)
        return jnp.concatenate(outs, axis=1)

    q = jnp.dot(xb, w_ref[:, 0:aw], preferred_element_type=F32)
    q_ref[...] = (rope(q) * Q_SCALE).astype(BF16)
    k = jnp.dot(xb, w_ref[:, aw:2 * aw], preferred_element_type=F32)
    k_ref[...] = rope(k).astype(BF16)
    v = jnp.dot(xb, w_ref[:, 2 * aw:3 * aw], preferred_element_type=F32)
    v_ref[...] = v.astype(BF16)
    rec_ref[...] = jnp.dot(xb, w_ref[:, 3 * aw:], preferred_element_type=F32)


def _rope_tables(seq):
    half = ROT_DIM // 2
    inv_freq = ROPE_THETA ** (-jnp.arange(half, dtype=F32) * 2.0 / ROT_DIM)
    ang = jnp.arange(seq).astype(F32)[:, None] * inv_freq[None, :]
    cos, sin = jnp.cos(ang), jnp.sin(ang)
    zeros = jnp.zeros((seq, HEAD_DIM - ROT_DIM), F32)
    zh = jnp.zeros((seq, half), F32)
    c64 = jnp.concatenate([cos, cos, zeros + 1.0], axis=1)
    lo64 = jnp.concatenate([zh, sin, zeros], axis=1)
    hi64 = jnp.concatenate([-sin, zh, zeros], axis=1)
    rep = LANES // HEAD_DIM
    return jnp.tile(c64, (1, rep)), jnp.tile(lo64, (1, rep)), jnp.tile(hi64, (1, rep))


def _in_proj(x2, w_bf, tables, seq, aw):
    t, d = x2.shape
    n = w_bf.shape[1]
    tm = min(512, seq)
    nseq = seq // tm
    tab_spec = pl.BlockSpec((tm, LANES), lambda i: (i % nseq, 0))
    return pl.pallas_call(
        functools.partial(_inproj_kernel, aw=aw),
        grid=(t // tm,),
        in_specs=[pl.BlockSpec((tm, d), lambda i: (i, 0)),
                  pl.BlockSpec((d, n), lambda i: (0, 0)),
                  tab_spec, tab_spec, tab_spec],
        out_specs=[pl.BlockSpec((tm, aw), lambda i: (i, 0)),
                   pl.BlockSpec((tm, aw), lambda i: (i, 0)),
                   pl.BlockSpec((tm, aw), lambda i: (i, 0)),
                   pl.BlockSpec((tm, n - 3 * aw), lambda i: (i, 0))],
        out_shape=[jax.ShapeDtypeStruct((t, aw), BF16),
                   jax.ShapeDtypeStruct((t, aw), BF16),
                   jax.ShapeDtypeStruct((t, aw), BF16),
                   jax.ShapeDtypeStruct((t, n - 3 * aw), F32)],
        compiler_params=_cparams(("parallel",)),
        name="in_proj",
    )(x2, w_bf, *tables)


def _attn_kernel(q_ref, k_ref, v_ref, o_ref, vt_ref, km_ref, bias_ref, m_ref, acc_ref, *, nb, npair):
    blk = MOBA_BLOCK
    i = pl.program_id(1)
    lane = lax.broadcasted_iota(jnp.int32, (1, LANES), 1)
    head_lanes = (lane < HEAD_DIM, lane >= HEAD_DIM)
    nt = (((1,), (1,)), ((), ()))
    heads = [(p, h) for p in range(npair) for h in range(2)]
    pair_lanes = lambda p: slice(p * LANES, (p + 1) * LANES)

    @pl.when(i == 0)
    def _prepare():
        def prep(j, _):
            off = pl.multiple_of(j * blk, blk)
            for p in range(npair):
                kb = k_ref[0, pl.ds(off, blk), pair_lanes(p)].astype(F32)
                km = jnp.sum(kb, axis=0, keepdims=True) * (1.0 / blk)
                km_ref[2 * p, pl.ds(j, 1), :] = jnp.where(head_lanes[0], km, 0.0)
                km_ref[2 * p + 1, pl.ds(j, 1), :] = jnp.where(head_lanes[1], km, 0.0)
                vt = v_ref[0, pl.ds(off, blk), pair_lanes(p)].astype(F32).T
                vt_ref[2 * p, 0:HEAD_DIM, pl.ds(off, blk)] = vt[0:HEAD_DIM].astype(BF16)
                vt_ref[2 * p + 1, 0:HEAD_DIM, pl.ds(off, blk)] = vt[HEAD_DIM:].astype(BF16)
            return 0

        lax.fori_loop(0, nb, prep, 0)
        vt_ref[:, HEAD_DIM:, :] = jnp.ones((2 * npair, VT_ROWS - HEAD_DIM, nb * blk), BF16)

    n_iota = lax.broadcasted_iota(jnp.int32, (nb, blk), 0)
    valid = n_iota < i
    qh = []
    for p, h in heads:
        q_pair = q_ref[0, :, pair_lanes(p)]
        qh.append(jnp.where(head_lanes[h], q_pair, jnp.zeros_like(q_pair)))
        g = lax.dot_general(km_ref[2 * p + h], q_pair.astype(F32), nt, preferred_element_type=F32)
        g = jnp.where(valid, g, -jnp.inf)
        bias = jnp.full((nb, blk), NEG_BIG, F32)
        for _ in range(min(MOBA_TOPK, nb)):
            top = jnp.max(g, axis=0, keepdims=True)
            first = jnp.min(jnp.where(g == top, n_iota, nb), axis=0, keepdims=True)
            hit = n_iota == first
            bias = jnp.where(hit & valid, 0.0, bias)
            g = jnp.where(hit, -jnp.inf, g)
        bias_ref[2 * p + h] = bias

    off_i = pl.multiple_of(i * blk, blk)
    kpos = lax.broadcasted_iota(jnp.int32, (blk, blk), 0)
    qpos = lax.broadcasted_iota(jnp.int32, (blk, blk), 1)
    causal = kpos <= qpos

    def scores(n, off, rows):
        kb = k_ref[0, pl.ds(off, rows), pair_lanes(heads[n][0])]
        return lax.dot_general(kb, qh[n], nt, preferred_element_type=F32)

    def interleave(stage_scores, stage_probs, stage_values):
        nh = len(heads)
        s_q = {n: stage_scores(n) for n in range(min(AHEAD, nh))}
        p_q = {}
        for t in range(nh + BEHIND):
            if t + AHEAD < nh:
                s_q[t + AHEAD] = stage_scores(t + AHEAD)
            if t < nh:
                p_q[t] = stage_probs(t, s_q.pop(t))
            if t >= BEHIND:
                stage_values(t - BEHIND, *p_q.pop(t - BEHIND))

    def own_probs(n, s):
        s = jnp.where(causal, s, NEG_BIG)
        m0 = jnp.max(s, axis=0, keepdims=True)
        m_ref[n] = m0
        return (jnp.exp2(s - m0).astype(BF16),)

    def own_values(n, pr):
        acc_ref[n] = jnp.dot(vt_ref[n, :, pl.ds(off_i, blk)], pr, preferred_element_type=F32)

    interleave(lambda n: scores(n, off_i, blk), own_probs, own_values)

    def body(jj, _):
        j0 = 2 * jj
        off = pl.multiple_of(j0 * blk, 2 * blk)

        def probs(n, s):
            b0 = bias_ref[n, pl.ds(j0, 1), :]
            b1 = bias_ref[n, pl.ds(j0 + 1, 1), :]
            s0, s1 = s[0:blk], s[blk:]
            mj = jnp.maximum(jnp.max(s0, axis=0, keepdims=True) + b0, jnp.max(s1, axis=0, keepdims=True) + b1)
            m_run = m_ref[n]
            m_new = jnp.maximum(m_run, mj)
            m_ref[n] = m_new
            alpha = jnp.exp2(m_run - m_new)
            p0 = jnp.exp2(s0 - (m_new - b0)).astype(BF16)
            p1 = jnp.exp2(s1 - (m_new - b1)).astype(BF16)
            return alpha, p0, p1

        def accumulate(n, alpha, p0, p1):
            pv = (jnp.dot(vt_ref[n, :, pl.ds(off, blk)], p0, preferred_element_type=F32)
                  + jnp.dot(vt_ref[n, :, pl.ds(off + blk, blk)], p1, preferred_element_type=F32))
            acc_ref[n] = alpha * acc_ref[n] + pv

        interleave(lambda n: scores(n, off, 2 * blk), probs, accumulate)
        return 0

    lax.fori_loop(0, (i + 1) // 2, body, 0)
    for p in range(npair):
        outs = []
        for h in range(2):
            acc = acc_ref[2 * p + h]
            outs.append(acc[0:HEAD_DIM] / acc[HEAD_DIM:HEAD_DIM + 1])
        o_ref[0, :, pair_lanes(p)] = jnp.concatenate(outs, axis=0).T


def _moba_attention(q, k, v):
    b, s, aw = q.shape
    nb = s // MOBA_BLOCK
    npair = aw // LANES
    nh = 2 * npair
    return pl.pallas_call(
        functools.partial(_attn_kernel, nb=nb, npair=npair),
        grid=(b, nb),
        in_specs=[pl.BlockSpec((1, MOBA_BLOCK, aw), lambda b_, i: (b_, i, 0)),
                  pl.BlockSpec((1, s, aw), lambda b_, i: (b_, 0, 0)),
                  pl.BlockSpec((1, s, aw), lambda b_, i: (b_, 0, 0))],
        out_specs=pl.BlockSpec((1, MOBA_BLOCK, aw), lambda b_, i: (b_, i, 0)),
        out_shape=jax.ShapeDtypeStruct((b, s, aw), F32),
        scratch_shapes=[pltpu.VMEM((nh, VT_ROWS, s), BF16),
                        pltpu.VMEM((nh, nb, LANES), F32),
                        pltpu.VMEM((nh, nb, MOBA_BLOCK), F32),
                        pltpu.VMEM((nh, 1, MOBA_BLOCK), F32),
                        pltpu.VMEM((nh, VT_ROWS, MOBA_BLOCK), F32)],
        compiler_params=_cparams(("parallel", "arbitrary")),
        name="moba_attn",
    )(q, k, v)


def _rglru_kernel(rec_ref, cw_ref, cb_ref, wg_ref, ba_ref, bi_ref, lam_ref, o_ref,
                  tail_ref, h_ref, a_ref, u_ref, hin_ref, xs_ref, *, rw):
    ts = rec_ref.shape[1]
    ng = ts // SUBLANES
    t_idx = pl.program_id(1)

    @pl.when(t_idx == 0)
    def _reset():
        tail_ref[...] = jnp.zeros_like(tail_ref)
        h_ref[...] = jnp.zeros_like(h_ref)

    xr = rec_ref[0, :, 0:rw]
    xs_ref[0:SUBLANES, :] = tail_ref[...]
    xs_ref[SUBLANES:, :] = xr
    tail_ref[...] = xr[ts - SUBLANES:, :]
    xc = xr * cw_ref[CONV_WIDTH - 1:CONV_WIDTH, :] + cb_ref[...]
    for d in range(1, CONV_WIDTH):
        xc = xc + xs_ref[SUBLANES - d:SUBLANES - d + ts, :] * cw_ref[CONV_WIDTH - 1 - d:CONV_WIDTH - d, :]

    pre = jnp.dot(xc.astype(BF16), wg_ref[...], preferred_element_type=F32)
    r = jax.nn.sigmoid(pre[:, 0:rw] + ba_ref[...])
    gi = jax.nn.sigmoid(pre[:, rw:] + bi_ref[...])
    lam = lam_ref[...]
    softplus_neg = jnp.maximum(-lam, 0.0) + jnp.log1p(jnp.exp(-jnp.abs(lam)))
    log_a = (-LRU_C * r) * softplus_neg
    a = jnp.exp(log_a)
    z = -jnp.tanh(log_a) * (a * a + 1.0)
    root = jnp.where(z > 0.0, z * lax.rsqrt(z), 0.0)
    u = root * (gi * xc)

    sub = lax.broadcasted_iota(jnp.int32, (ts, 1), 0) % SUBLANES
    d = 1
    while d < SUBLANES:
        live = sub >= d
        u = jnp.where(live, a * pltpu.roll(u, d, 0) + u, u)
        a = jnp.where(live, a * pltpu.roll(a, d, 0), a)
        d *= 2
    nplane = rw // LANES
    last_rows = pl.ds(SUBLANES - 1, ng, stride=SUBLANES)
    for c in range(nplane):
        a_ref[c] = a[:, c * LANES:(c + 1) * LANES]
        u_ref[c] = u[:, c * LANES:(c + 1) * LANES]
    ag = jnp.concatenate([a_ref[c, last_rows, :] for c in range(nplane)], axis=1)
    ug = jnp.concatenate([u_ref[c, last_rows, :] for c in range(nplane)], axis=1)
    grow = lax.broadcasted_iota(jnp.int32, (ng, 1), 0)
    d = 1
    while d < ng:
        live = grow >= d
        ug = jnp.where(live, ag * pltpu.roll(ug, d, 0) + ug, ug)
        ag = jnp.where(live, ag * pltpu.roll(ag, d, 0), ag)
        d *= 2
    h0 = h_ref[...]
    hg = ag * h0 + ug
    hin_ref[...] = jnp.where(grow >= 1, pltpu.roll(hg, 1, 0), h0)
    h_ref[...] = hg[ng - 1:, :]
    for g in range(ng):
        rows = slice(g * SUBLANES, (g + 1) * SUBLANES)
        a_g = jnp.concatenate([a_ref[c, rows, :] for c in range(nplane)], axis=1)
        u_g = jnp.concatenate([u_ref[c, rows, :] for c in range(nplane)], axis=1)
        h = a_g * hin_ref[g:g + 1, :] + u_g
        o_ref[0, rows, :] = h * jax.nn.gelu(rec_ref[0, rows, rw:])


def _rglru(rec, conv_w, conv_b, w_gates_bf, b_a, b_i, lam):
    b, s, two_rw = rec.shape
    rw = two_rw // 2
    ts = min(512, s)
    full = lambda shape: pl.BlockSpec(shape, lambda b_, t: (0,) * len(shape))
    return pl.pallas_call(
        functools.partial(_rglru_kernel, rw=rw),
        grid=(b, s // ts),
        in_specs=[pl.BlockSpec((1, ts, two_rw), lambda b_, t: (b_, t, 0)),
                  full((CONV_WIDTH, rw)), full((1, rw)), full((rw, two_rw)),
                  full((1, rw)), full((1, rw)), full((1, rw))],
        out_specs=pl.BlockSpec((1, ts, rw), lambda b_, t: (b_, t, 0)),
        out_shape=jax.ShapeDtypeStruct((b, s, rw), F32),
        scratch_shapes=[pltpu.VMEM((SUBLANES, rw), F32), pltpu.VMEM((1, rw), F32),
                        pltpu.VMEM((rw // LANES, ts, LANES), F32), pltpu.VMEM((rw // LANES, ts, LANES), F32),
                        pltpu.VMEM((ts // SUBLANES, rw), F32), pltpu.VMEM((ts + SUBLANES, rw), F32)],
        compiler_params=_cparams(("parallel", "arbitrary")),
        name="rglru",
    )(rec, conv_w, conv_b, w_gates_bf, b_a, b_i, lam)


def _block_diag(w):
    n, c, d = w.shape
    eye = jnp.eye(n, dtype=w.dtype)
    return (w[:, :, None, :] * eye[:, None, :, None]).reshape(n * c, n * d)


def _layer_norm(z, g, b):
    mu = jnp.mean(z, axis=-1, keepdims=True)
    zc = z - mu
    var = jnp.mean(zc * zc, axis=-1, keepdims=True)
    return zc * lax.rsqrt(var + LN_EPS) * g + b


def _rms(z, g):
    return z * lax.rsqrt(jnp.mean(z * z, axis=-1, keepdims=True) + RMS_EPS) * g


def _route(logits):
    lane = lax.broadcasted_iota(jnp.int32, logits.shape, 1).astype(F32)
    neg_inf = -jnp.inf
    first_lane = lambda mask: jnp.min(jnp.where(mask, lane, float(LANES)), axis=-1, keepdims=True)
    gl = jnp.where(lane < N_GROUPS, logits, neg_inf)
    gmax = jnp.max(gl, axis=-1, keepdims=True)
    gsum = jnp.sum(jnp.exp(gl - gmax), axis=-1, keepdims=True)
    g_p = 1.0 / gsum
    g_idx = first_lane(gl == gmax)
    lo = N_GROUPS + EXPERTS_PER_GROUP * g_idx
    emask = (lane >= lo) & (lane < lo + EXPERTS_PER_GROUP)
    el = jnp.where(emask, logits, neg_inf)
    emax = jnp.max(el, axis=-1, keepdims=True)
    ee = jnp.exp(el - emax)
    ep = ee / jnp.sum(ee, axis=-1, keepdims=True)
    e1 = jnp.max(ep, axis=-1, keepdims=True)
    i1 = first_lane(emask & (ep == e1))
    rest = emask & (lane != i1)
    ep2 = jnp.where(rest, ep, neg_inf)
    e2 = jnp.max(ep2, axis=-1, keepdims=True)
    i2 = first_lane(rest & (ep2 == e2))
    denom = e1 + e2
    w1 = g_p * e1 / denom
    w2 = g_p * e2 / denom
    comb = jnp.where(lane == i1, w1, 0.0) + jnp.where(lane == i2, w2, 0.0)
    return jnp.where(lane == 0, g_idx, comb)


def _outproj_kernel(attn_ref, rec_ref, x_ref, ga_ref, gr_ref, w_ref, g_ref, b_ref, wr_ref, br_ref,
                    x1_ref, *, alpha, d, n_sub):
    sub = x_ref.shape[0] // n_sub
    rows = [slice(k * sub, (k + 1) * sub) for k in range(n_sub)]
    mixed = [jnp.concatenate([_rms(attn_ref[r, :], ga_ref[...]), _rms(rec_ref[r, :], gr_ref[...])],
                             axis=-1).astype(BF16) for r in rows]
    h = [jnp.dot(m, w_ref[...], preferred_element_type=F32) for m in mixed]
    logits = []
    for r, hk in zip(rows, h):
        x1 = _layer_norm(alpha * x_ref[r, :] + hk, g_ref[...], b_ref[...])
        x1_ref[r, 0:d] = x1
        logits.append(jnp.dot(x1.astype(BF16), wr_ref[...], preferred_element_type=F32) + br_ref[...])
    for r, lg in zip(rows, logits):
        x1_ref[r, d:] = _route(lg)


def _out_proj(attn2, rec2, x2, g_a, g_r, w_out_bf, ln_g, ln_b, w_router_bf, b_router, alpha):
    t, d = x2.shape
    aw, rw = attn2.shape[1], rec2.shape[1]
    tm = min(1024, t)
    n_sub = 2 if tm % 1024 == 0 else 1
    row = lambda w: pl.BlockSpec((tm, w), lambda i: (i, 0))
    full = lambda shape: pl.BlockSpec(shape, lambda i: (0,) * len(shape))
    return pl.pallas_call(
        functools.partial(_outproj_kernel, alpha=alpha, d=d, n_sub=n_sub),
        grid=(t // tm,),
        in_specs=[row(aw), row(rw), row(d), full((1, aw)), full((1, rw)), full((aw + rw, d)),
                  full((1, d)), full((1, d)), full((d, LANES)), full((1, LANES))],
        out_specs=row(d + LANES),
        out_shape=jax.ShapeDtypeStruct((t, d + LANES), F32),
        compiler_params=_cparams(("parallel",)),
        name="out_proj_ln_router",
    )(attn2, rec2, x2, g_a, g_r, w_out_bf, ln_g, ln_b, w_router_bf, b_router)


def _dispatch_plan(gidx, tm):
    t = gidx.shape[0]
    n_tiles = t // tm + N_GROUPS
    counts = jnp.sum((gidx[:, None] == jnp.arange(N_GROUPS)[None, :]).astype(jnp.int32), axis=0)
    tiles_g = (counts + tm - 1) // tm
    tile_end = jnp.cumsum(tiles_g)
    tile_start = tile_end - tiles_g
    n_used = tile_end[-1]
    sorted_start = jnp.cumsum(counts) - counts
    order = jnp.argsort(gidx, stable=True).astype(jnp.int32)
    tile = jnp.arange(n_tiles, dtype=jnp.int32)
    tg = jnp.sum((tile_end[None, :] <= tile[:, None]).astype(jnp.int32), axis=1)
    used = tile < n_used
    tg = jnp.where(used, tg, tg[n_used - 1])
    row_off = (tile - tile_start[tg]) * tm
    nvalid = jnp.where(used, jnp.clip(counts[tg] - row_off, 0, tm), 0).astype(jnp.int32)
    r = jnp.arange(tm, dtype=jnp.int32)[None, :]
    pos = sorted_start[tg][:, None] + row_off[:, None] + jnp.minimum(r, nvalid[:, None] - 1)
    src = order[jnp.clip(pos, 0, t - 1)]
    return src.reshape(-1), tg.astype(jnp.int32), nvalid


def _moe_kernel(src_ref, tg_ref, nv_ref, x_hbm, wg_ref, wu_ref, wd_ref, g_ref, b_ref, out_hbm,
                xbuf0, xbuf1, obuf0, obuf1, gsem, ssem, *, alpha, tm, d, n_tiles):
    i = pl.program_id(0)
    nxt = jnp.minimum(i + 1, n_tiles - 1)
    prv = jnp.maximum(i - 1, 0)
    nv_prv = jnp.where(i > 0, nv_ref[prv], 0)
    prv_full = nv_prv == tm

    def gather_row(tile, r, xbuf, sem):
        tok = src_ref[tile * tm + r]
        return pltpu.make_async_copy(x_hbm.at[pl.ds(tok, 1), :], xbuf.at[pl.ds(r, 1), :], sem)

    def scatter_row(tile, r, obuf, sem):
        tok = src_ref[tile * tm + r]
        return pltpu.make_async_copy(obuf.at[pl.ds(r, 1), :], out_hbm.at[pl.ds(tok, 1), :], sem)

    gather_tile = lambda xbuf, sem: pltpu.make_async_copy(x_hbm.at[pl.ds(0, tm), :], xbuf, sem)
    scatter_tile = lambda obuf, sem: pltpu.make_async_copy(obuf, out_hbm.at[pl.ds(0, tm), :], sem)

    def for_rows(n, fn, unroll):
        def body(r, _):
            fn(r)
            return 0
        lax.fori_loop(0, n, body, 0, unroll=unroll)

    def step(x_cur, x_nxt, o_cur, o_prv, g_cur, g_nxt, s_cur, s_prv):
        @pl.when(i == 0)
        def _prologue():
            for_rows(tm, lambda r: gather_row(0, r, x_cur, g_cur).start(), 8)

        @pl.when(prv_full)
        def _issue_full():
            for r in range(tm):
                gather_row(nxt, r, x_nxt, g_nxt).start()
                scatter_row(prv, r, o_prv, s_prv).start()

        @pl.when(jnp.logical_not(prv_full))
        def _issue_partial():
            for r in range(tm):
                gather_row(nxt, r, x_nxt, g_nxt).start()
            for_rows(nv_prv, lambda r: scatter_row(prv, r, o_prv, s_prv).start(), 1)

        gather_tile(x_cur, g_cur).wait()
        x = x_cur[:, 0:d]
        route = x_cur[:, d:]
        xb = x.astype(BF16)
        lane = lax.broadcasted_iota(jnp.int32, route.shape, 1)
        base = N_GROUPS + EXPERTS_PER_GROUP * tg_ref[i]
        y = jnp.zeros((tm, d), F32)
        for e in range(EXPERTS_PER_GROUP):
            gate = jnp.dot(xb, wg_ref[e], preferred_element_type=F32)
            up = jnp.dot(xb, wu_ref[e], preferred_element_type=F32)
            hidden = (jax.nn.silu(gate) * up).astype(BF16)
            w_e = jnp.sum(jnp.where(lane == base + e, route, 0.0), axis=-1, keepdims=True)
            y = y + w_e * jnp.dot(hidden, wd_ref[e], preferred_element_type=F32)
        o_cur[...] = _layer_norm(alpha * x + y, g_ref[...], b_ref[...])

        @pl.when(prv_full)
        def _drain_full():
            scatter_tile(o_prv, s_prv).wait()

        @pl.when(jnp.logical_not(prv_full))
        def _drain_partial():
            for_rows(nv_prv, lambda r: scatter_row(prv, r, o_prv, s_prv).wait(), 1)

        @pl.when(i == n_tiles - 1)
        def _epilogue():
            gather_tile(x_nxt, g_nxt).wait()
            for_rows(nv_ref[i], lambda r: scatter_row(i, r, o_cur, s_cur).start(), 1)
            for_rows(nv_ref[i], lambda r: scatter_row(i, r, o_cur, s_cur).wait(), 1)

    @pl.when(i % 2 == 0)
    def _even():
        step(xbuf0, xbuf1, obuf0, obuf1, gsem.at[0], gsem.at[1], ssem.at[0], ssem.at[1])

    @pl.when(i % 2 == 1)
    def _odd():
        step(xbuf1, xbuf0, obuf1, obuf0, gsem.at[1], gsem.at[0], ssem.at[1], ssem.at[0])


def _moe(x1e, t, wg_bf, wu_bf, wd_bf, ln_g, ln_b, alpha, d):
    n_experts, _, de = wg_bf.shape
    tm = min(512, t)
    gidx = x1e[:, d].astype(jnp.int32)
    src, tg, nvalid = _dispatch_plan(gidx, tm)
    n_tiles = tg.shape[0]
    epg = EXPERTS_PER_GROUP
    full = lambda shape: pl.BlockSpec(shape, lambda i, *_: (0,) * len(shape))
    grid_spec = pltpu.PrefetchScalarGridSpec(
        num_scalar_prefetch=3,
        grid=(n_tiles,),
        in_specs=[pl.BlockSpec(memory_space=pl.ANY),
                  pl.BlockSpec((epg, d, de), lambda i, src_, tg_, nv_: (tg_[i], 0, 0)),
                  pl.BlockSpec((epg, d, de), lambda i, src_, tg_, nv_: (tg_[i], 0, 0)),
                  pl.BlockSpec((epg, de, d), lambda i, src_, tg_, nv_: (tg_[i], 0, 0)),
                  full((1, d)), full((1, d))],
        out_specs=pl.BlockSpec(memory_space=pl.ANY),
        scratch_shapes=[pltpu.VMEM((tm, d + LANES), F32), pltpu.VMEM((tm, d + LANES), F32),
                        pltpu.VMEM((tm, d), F32), pltpu.VMEM((tm, d), F32),
                        pltpu.SemaphoreType.DMA((2,)),
                        pltpu.SemaphoreType.DMA((2,))],
    )
    return pl.pallas_call(
        functools.partial(_moe_kernel, alpha=alpha, tm=tm, d=d, n_tiles=n_tiles),
        grid_spec=grid_spec,
        out_shape=jax.ShapeDtypeStruct((t, d), F32),
        compiler_params=_cparams(("arbitrary",)),
        name="moe_ln",
    )(src, tg, nvalid, x1e, wg_bf, wu_bf, wd_bf, ln_g, ln_b)


def kernel(x, w_in, conv_w, conv_b, w_rg_a, b_rg_a, w_rg_i, b_rg_i, lru_lambda, g_attn_norm, g_rec_norm, w_out,
           ln1_g, ln1_b, w_router_group, b_router_group, w_router_expert, b_router_expert, w_gate, w_up, w_down,
           ln2_g, ln2_b):
    b, s, d = x.shape
    depth = w_in.shape[0]
    rw = conv_w.shape[2]
    aw = (w_in.shape[2] - 2 * rw) // 3
    alpha = (2 * depth) ** 0.25
    tables = _rope_tables(s)
    row = lambda v: v.reshape(1, -1)

    x2 = x.reshape(b * s, d)
    for l in range(depth):
        q, k, v, rec = _in_proj(x2, w_in[l].astype(BF16), tables, s, aw)
        attn = _moba_attention(q.reshape(b, s, aw), k.reshape(b, s, aw), v.reshape(b, s, aw))
        w_gates = jnp.concatenate([_block_diag(w_rg_a[l]), _block_diag(w_rg_i[l])], axis=1).astype(BF16)
        rec_out = _rglru(rec.reshape(b, s, 2 * rw), conv_w[l], row(conv_b[l]), w_gates,
                         row(b_rg_a[l]), row(b_rg_i[l]), row(lru_lambda[l]))
        n_route = N_GROUPS + N_GROUPS * EXPERTS_PER_GROUP
        w_router = jnp.concatenate([w_router_group[l], w_router_expert[l]], axis=1)
        w_router = jnp.pad(w_router, ((0, 0), (0, LANES - n_route))).astype(BF16)
        b_router = jnp.pad(jnp.concatenate([b_router_group[l], b_router_expert[l]]), (0, LANES - n_route))
        x1e = _out_proj(attn.reshape(b * s, aw), rec_out.reshape(b * s, rw), x2,
                        row(g_attn_norm[l]), row(g_rec_norm[l]), w_out[l].astype(BF16),
                        row(ln1_g[l]), row(ln1_b[l]), w_router, row(b_router), alpha)
        x2 = _moe(x1e, b * s, w_gate[l].astype(BF16), w_up[l].astype(BF16), w_down[l].astype(BF16),
                  row(ln2_g[l]), row(ln2_b[l]), alpha, d)
    return x2.reshape(b, s, d)
```

```python
import functools

import jax
import jax.numpy as jnp
from jax import lax
from jax.experimental import pallas as pl
from jax.experimental.pallas import tpu as pltpu

F32 = jnp.float32
BF16 = jnp.bfloat16

HEAD_DIM = 64
ROT_DIM = HEAD_DIM // 4
ROPE_THETA = 500000.0
MOBA_BLOCK = 256
MOBA_TOPK = 3
CONV_WIDTH = 4
LRU_C = 8.0
N_GROUPS = 4
EXPERTS_PER_GROUP = 4
LN_EPS = 1e-5
RMS_EPS = 1e-6

LANES = 128
SUBLANES = 8
NEG_BIG = -1e30
LOG2E = 1.4426950408889634
Q_SCALE = HEAD_DIM ** -0.5 * LOG2E
AHEAD, BEHIND = 3, 1
VT_ROWS = HEAD_DIM + 16
VMEM_LIMIT = 56 * 1024 * 1024


def _cparams(sem):
    return pltpu.CompilerParams(dimension_semantics=sem, vmem_limit_bytes=VMEM_LIMIT)


def _inproj_kernel(x_ref, w_ref, c_ref, s1_ref, s2_ref, q_ref, k_ref, v_ref, rec_ref, *, aw):
    xb = x_ref[...].astype(BF16)
    cos, sin_lo, sin_hi = c_ref[...], s1_ref[...], s2_ref[...]

    def rope(t):
        outs = []
        for c in range(aw // LANES):
            tc = t[:, c * LANES:(c + 1) * LANES]
            outs.append(tc * cos + pltpu.roll(tc, 8, 1) * sin_lo + pltpu.roll(tc, LANES - 8, 1) * sin_hi)
        return jnp.concatenate(outs, axis=1)

    q = jnp.dot(xb, w_ref[:, 0:aw], preferred_element_type=F32)
    q_ref[...] = (rope(q) * Q_SCALE).astype(BF16)
    k = jnp.dot(xb, w_ref[:, aw:2 * aw], preferred_element_type=F32)
    k_ref[...] = rope(k).astype(BF16)
    v = jnp.dot(xb, w_ref[:, 2 * aw:3 * aw], preferred_element_type=F32)
    v_ref[...] = v.astype(BF16)
    rec_ref[...] = jnp.dot(xb, w_ref[:, 3 * aw:], preferred_element_type=F32)


def _rope_tables(seq):
    half = ROT_DIM // 2
    inv_freq = ROPE_THETA ** (-jnp.arange(half, dtype=F32) * 2.0 / ROT_DIM)
    ang = jnp.arange(seq).astype(F32)[:, None] * inv_freq[None, :]
    cos, sin = jnp.cos(ang), jnp.sin(ang)
    zeros = jnp.zeros((seq, HEAD_DIM - ROT_DIM), F32)
    zh = jnp.zeros((seq, half), F32)
    c64 = jnp.concatenate([cos, cos, zeros + 1.0], axis=1)
    lo64 = jnp.concatenate([zh, sin, zeros], axis=1)
    hi64 = jnp.concatenate([-sin, zh, zeros], axis=1)
    rep = LANES // HEAD_DIM
    return jnp.tile(c64, (1, rep)), jnp.tile(lo64, (1, rep)), jnp.tile(hi64, (1, rep))


def _in_proj(x2, w_bf, tables, seq, aw):
    t, d = x2.shape
    n = w_bf.shape[1]
    tm = min(512, seq)
    nseq = seq // tm
    tab_spec = pl.BlockSpec((tm, LANES), lambda i: (i % nseq, 0))
    return pl.pallas_call(
        functools.partial(_inproj_kernel, aw=aw),
        grid=(t // tm,),
        in_specs=[pl.BlockSpec((tm, d), lambda i: (i, 0)),
                  pl.BlockSpec((d, n), lambda i: (0, 0)),
                  tab_spec, tab_spec, tab_spec],
        out_specs=[pl.BlockSpec((tm, aw), lambda i: (i, 0)),
                   pl.BlockSpec((tm, aw), lambda i: (i, 0)),
                   pl.BlockSpec((tm, aw), lambda i: (i, 0)),
                   pl.BlockSpec((tm, n - 3 * aw), lambda i: (i, 0))],
        out_shape=[jax.ShapeDtypeStruct((t, aw), BF16),
                   jax.ShapeDtypeStruct((t, aw), BF16),
                   jax.ShapeDtypeStruct((t, aw), BF16),
                   jax.ShapeDtypeStruct((t, n - 3 * aw), F32)],
        compiler_params=_cparams(("parallel",)),
        name="in_proj",
    )(x2, w_bf, *tables)


def _attn_kernel(q_ref, k_ref, v_ref, o_ref, vt_ref, km_ref, bias_ref, m_ref, acc_ref, *, nb, npair):
    blk = MOBA_BLOCK
    i = pl.program_id(1)
    lane = lax.broadcasted_iota(jnp.int32, (1, LANES), 1)
    head_lanes = (lane < HEAD_DIM, lane >= HEAD_DIM)
    nt = (((1,), (1,)), ((), ()))
    heads = [(p, h) for p in range(npair) for h in range(2)]
    pair_lanes = lambda p: slice(p * LANES, (p + 1) * LANES)

    @pl.when(i == 0)
    def _prepare():
        def prep(j, _):
            off = pl.multiple_of(j * blk, blk)
            for p in range(npair):
                kb = k_ref[0, pl.ds(off, blk), pair_lanes(p)].astype(F32)
                km = jnp.sum(kb, axis=0, keepdims=True) * (1.0 / blk)
                km_ref[2 * p, pl.ds(j, 1), :] = jnp.where(head_lanes[0], km, 0.0)
                km_ref[2 * p + 1, pl.ds(j, 1), :] = jnp.where(head_lanes[1], km, 0.0)
                vt = v_ref[0, pl.ds(off, blk), pair_lanes(p)].astype(F32).T
                vt_ref[2 * p, 0:HEAD_DIM, pl.ds(off, blk)] = vt[0:HEAD_DIM].astype(BF16)
                vt_ref[2 * p + 1, 0:HEAD_DIM, pl.ds(off, blk)] = vt[HEAD_DIM:].astype(BF16)
            return 0

        lax.fori_loop(0, nb, prep, 0)
        vt_ref[:, HEAD_DIM:, :] = jnp.ones((2 * npair, VT_ROWS - HEAD_DIM, nb * blk), BF16)

    n_iota = lax.broadcasted_iota(jnp.int32, (nb, blk), 0)
    valid = n_iota < i
    qh = []
    for p, h in heads:
        q_pair = q_ref[0, :, pair_lanes(p)]
        qh.append(jnp.where(head_lanes[h], q_pair, jnp.zeros_like(q_pair)))
        g = lax.dot_general(km_ref[2 * p + h], q_pair.astype(F32), nt, preferred_element_type=F32)
        g = jnp.where(valid, g, -jnp.inf)
        bias = jnp.full((nb, blk), NEG_BIG, F32)
        for _ in range(min(MOBA_TOPK, nb)):
            top = jnp.max(g, axis=0, keepdims=True)
            first = jnp.min(jnp.where(g == top, n_iota, nb), axis=0, keepdims=True)
            hit = n_iota == first
            bias = jnp.where(hit & valid, 0.0, bias)
            g = jnp.where(hit, -jnp.inf, g)
        bias_ref[2 * p + h] = bias

    off_i = pl.multiple_of(i * blk, blk)
    kpos = lax.broadcasted_iota(jnp.int32, (blk, blk), 0)
    qpos = lax.broadcasted_iota(jnp.int32, (blk, blk), 1)
    causal = kpos <= qpos

    def scores(n, off, rows):
        kb = k_ref[0, pl.ds(off, rows), pair_lanes(heads[n][0])]
        return lax.dot_general(kb, qh[n], nt, preferred_element_type=F32)

    def interleave(stage_scores, stage_probs, stage_values):
        nh = len(heads)
        s_q = {n: stage_scores(n) for n in range(min(AHEAD, nh))}
        p_q = {}
        for t in range(nh + BEHIND):
            if t + AHEAD < nh:
                s_q[t + AHEAD] = stage_scores(t + AHEAD)
            if t < nh:
                p_q[t] = stage_probs(t, s_q.pop(t))
            if t >= BEHIND:
                stage_values(t - BEHIND, *p_q.pop(t - BEHIND))

    def own_probs(n, s):
        s = jnp.where(causal, s, NEG_BIG)
        m0 = jnp.max(s, axis=0, keepdims=True)
        m_ref[n] = m0
        return (jnp.exp2(s - m0).astype(BF16),)

    def own_values(n, pr):
        acc_ref[n] = jnp.dot(vt_ref[n, :, pl.ds(off_i, blk)], pr, preferred_element_type=F32)

    interleave(lambda n: scores(n, off_i, blk), own_probs, own_values)

    def body(jj, _):
        j0 = 2 * jj
        off = pl.multiple_of(j0 * blk, 2 * blk)

        def probs(n, s):
            b0 = bias_ref[n, pl.ds(j0, 1), :]
            b1 = bias_ref[n, pl.ds(j0 + 1, 1), :]
            s0, s1 = s[0:blk], s[blk:]
            mj = jnp.maximum(jnp.max(s0, axis=0, keepdims=True) + b0, jnp.max(s1, axis=0, keepdims=True) + b1)
            m_run = m_ref[n]
            m_new = jnp.maximum(m_run, mj)
            m_ref[n] = m_new
            alpha = jnp.exp2(m_run - m_new)
            p0 = jnp.exp2(s0 - (m_new - b0)).astype(BF16)
            p1 = jnp.exp2(s1 - (m_new - b1)).astype(BF16)
            return alpha, p0, p1

        def accumulate(n, alpha, p0, p1):
            pv = (jnp.dot(vt_ref[n, :, pl.ds(off, blk)], p0, preferred_element_type=F32)
                  + jnp.dot(vt_ref[n, :, pl.ds(off + blk, blk)], p1, preferred_element_type=F32))
            acc_ref[n] = alpha * acc_ref[n] + pv

        interleave(lambda n: scores(n, off, 2 * blk), probs, accumulate)
        return 0

    lax.fori_loop(0, (i + 1) // 2, body, 0)
    for p in range(npair):
        outs = []
        for h in range(2):
            acc = acc_ref[2 * p + h]
            outs.append(acc[0:HEAD_DIM] / acc[HEAD_DIM:HEAD_DIM + 1])
        o_ref[0, :, pair_lanes(p)] = jnp.concatenate(outs, axis=0).T.astype(o_ref.dtype)


def _moba_attention(q, k, v):
    b, s, aw = q.shape
    nb = s // MOBA_BLOCK
    npair = aw // LANES
    nh = 2 * npair
    return pl.pallas_call(
        functools.partial(_attn_kernel, nb=nb, npair=npair),
        grid=(b, nb),
        in_specs=[pl.BlockSpec((1, MOBA_BLOCK, aw), lambda b_, i: (b_, i, 0)),
                  pl.BlockSpec((1, s, aw), lambda b_, i: (b_, 0, 0)),
                  pl.BlockSpec((1, s, aw), lambda b_, i: (b_, 0, 0))],
        out_specs=pl.BlockSpec((1, MOBA_BLOCK, aw), lambda b_, i: (b_, i, 0)),
        out_shape=jax.ShapeDtypeStruct((b, s, aw), BF16),
        scratch_shapes=[pltpu.VMEM((nh, VT_ROWS, s), BF16),
                        pltpu.VMEM((nh, nb, LANES), F32),
                        pltpu.VMEM((nh, nb, MOBA_BLOCK), F32),
                        pltpu.VMEM((nh, 1, MOBA_BLOCK), F32),
                        pltpu.VMEM((nh, VT_ROWS, MOBA_BLOCK), F32)],
        compiler_params=_cparams(("parallel", "arbitrary")),
        name="moba_attn",
    )(q, k, v)


def _rglru_kernel(rec_ref, cw_ref, cb_ref, wg_ref, ba_ref, bi_ref, lam_ref, o_ref,
                  tail_ref, h_ref, a_ref, u_ref, hin_ref, xs_ref, *, rw):
    ts = rec_ref.shape[1]
    t_idx = pl.program_id(1)

    @pl.when(t_idx == 0)
    def _reset():
        tail_ref[...] = jnp.zeros_like(tail_ref)
        h_ref[...] = jnp.zeros_like(h_ref)

    xr = rec_ref[0, :, 0:rw]
    xs_ref[0:SUBLANES, :] = tail_ref[...]
    xs_ref[SUBLANES:, :] = xr
    tail_ref[...] = xr[ts - SUBLANES:, :]
    xc = xr * cw_ref[CONV_WIDTH - 1:CONV_WIDTH, :] + cb_ref[...]
    for d in range(1, CONV_WIDTH):
        xc = xc + xs_ref[SUBLANES - d:SUBLANES - d + ts, :] * cw_ref[CONV_WIDTH - 1 - d:CONV_WIDTH - d, :]

    pre = jnp.dot(xc.astype(BF16), wg_ref[...], preferred_element_type=F32)
    r = jax.nn.sigmoid(pre[:, 0:rw] + ba_ref[...])
    gi = jax.nn.sigmoid(pre[:, rw:] + bi_ref[...])
    lam = lam_ref[...]
    softplus_neg = jnp.maximum(-lam, 0.0) + jnp.log1p(jnp.exp(-jnp.abs(lam)))
    log_a = (-LRU_C * r) * softplus_neg
    a = jnp.exp(log_a)
    z = -jnp.tanh(log_a) * (a * a + 1.0)
    root = jnp.where(z > 0.0, z * lax.rsqrt(z), 0.0)
    u = root * (gi * xc)

    ng = ts // SUBLANES
    sub = lax.broadcasted_iota(jnp.int32, (ts, 1), 0) % SUBLANES
    d = 1
    while d < SUBLANES:
        live = sub >= d
        u = jnp.where(live, a * pltpu.roll(u, d, 0) + u, u)
        a = jnp.where(live, a * pltpu.roll(a, d, 0), a)
        d *= 2
    nplane = rw // LANES
    last_rows = pl.ds(SUBLANES - 1, ng, stride=SUBLANES)
    for c in range(nplane):
        a_ref[c] = a[:, c * LANES:(c + 1) * LANES]
        u_ref[c] = u[:, c * LANES:(c + 1) * LANES]
    ag = jnp.concatenate([a_ref[c, last_rows, :] for c in range(nplane)], axis=1)
    ug = jnp.concatenate([u_ref[c, last_rows, :] for c in range(nplane)], axis=1)
    grow = lax.broadcasted_iota(jnp.int32, (ng, 1), 0)
    d = 1
    while d < ng:
        live = grow >= d
        ug = jnp.where(live, ag * pltpu.roll(ug, d, 0) + ug, ug)
        ag = jnp.where(live, ag * pltpu.roll(ag, d, 0), ag)
        d *= 2
    h0 = h_ref[...]
    hg = ag * h0 + ug
    hin_ref[...] = jnp.where(grow >= 1, pltpu.roll(hg, 1, 0), h0)
    h_ref[...] = hg[ng - 1:, :]
    for g2 in range(0, ng, 2):
        hs = []
        for g in (g2, g2 + 1):
            rows = slice(g * SUBLANES, (g + 1) * SUBLANES)
            a_g = jnp.concatenate([a_ref[c, rows, :] for c in range(nplane)], axis=1)
            u_g = jnp.concatenate([u_ref[c, rows, :] for c in range(nplane)], axis=1)
            hs.append(a_g * hin_ref[g:g + 1, :] + u_g)
        rows = slice(g2 * SUBLANES, (g2 + 2) * SUBLANES)
        gated = jnp.concatenate(hs, axis=0) * jax.nn.gelu(rec_ref[0, rows, rw:])
        o_ref[0, rows, :] = gated.astype(o_ref.dtype)


def _rglru(rec, conv_w, conv_b, w_gates_bf, b_a, b_i, lam):
    b, s, two_rw = rec.shape
    rw = two_rw // 2
    ts = min(512, s)
    full = lambda shape: pl.BlockSpec(shape, lambda b_, t: (0,) * len(shape))
    return pl.pallas_call(
        functools.partial(_rglru_kernel, rw=rw),
        grid=(b, s // ts),
        in_specs=[pl.BlockSpec((1, ts, two_rw), lambda b_, t: (b_, t, 0)),
                  full((CONV_WIDTH, rw)), full((1, rw)), full((rw, two_rw)),
                  full((1, rw)), full((1, rw)), full((1, rw))],
        out_specs=pl.BlockSpec((1, ts, rw), lambda b_, t: (b_, t, 0)),
        out_shape=jax.ShapeDtypeStruct((b, s, rw), BF16),
        scratch_shapes=[pltpu.VMEM((SUBLANES, rw), F32), pltpu.VMEM((1, rw), F32),
                        pltpu.VMEM((rw // LANES, ts, LANES), F32), pltpu.VMEM((rw // LANES, ts, LANES), F32),
                        pltpu.VMEM((ts // SUBLANES, rw), F32), pltpu.VMEM((ts + SUBLANES, rw), F32)],
        compiler_params=_cparams(("parallel", "arbitrary")),
        name="rglru",
    )(rec, conv_w, conv_b, w_gates_bf, b_a, b_i, lam)


def _block_diag(w):
    n, c, d = w.shape
    eye = jnp.eye(n, dtype=w.dtype)
    return (w[:, :, None, :] * eye[:, None, :, None]).reshape(n * c, n * d)


def _layer_norm(z, g, b):
    mu = jnp.mean(z, axis=-1, keepdims=True)
    zc = z - mu
    var = jnp.mean(zc * zc, axis=-1, keepdims=True)
    return zc * lax.rsqrt(var + LN_EPS) * g + b


def _rms(z, g):
    return z * lax.rsqrt(jnp.mean(z * z, axis=-1, keepdims=True) + RMS_EPS) * g


def _route(logits):
    lane = lax.broadcasted_iota(jnp.int32, logits.shape, 1).astype(F32)
    neg_inf = -jnp.inf
    first_lane = lambda mask: jnp.min(jnp.where(mask, lane, float(LANES)), axis=-1, keepdims=True)
    gl = jnp.where(lane < N_GROUPS, logits, neg_inf)
    gmax = jnp.max(gl, axis=-1, keepdims=True)
    gsum = jnp.sum(jnp.exp(gl - gmax), axis=-1, keepdims=True)
    g_p = 1.0 / gsum
    g_idx = first_lane(gl == gmax)
    lo = N_GROUPS + EXPERTS_PER_GROUP * g_idx
    emask = (lane >= lo) & (lane < lo + EXPERTS_PER_GROUP)
    el = jnp.where(emask, logits, neg_inf)
    emax = jnp.max(el, axis=-1, keepdims=True)
    ee = jnp.exp(el - emax)
    ep = ee / jnp.sum(ee, axis=-1, keepdims=True)
    e1 = jnp.max(ep, axis=-1, keepdims=True)
    i1 = first_lane(emask & (ep == e1))
    rest = emask & (lane != i1)
    ep2 = jnp.where(rest, ep, neg_inf)
    e2 = jnp.max(ep2, axis=-1, keepdims=True)
    i2 = first_lane(rest & (ep2 == e2))
    denom = e1 + e2
    w1 = g_p * e1 / denom
    w2 = g_p * e2 / denom
    comb = jnp.where(lane == i1, w1, 0.0) + jnp.where(lane == i2, w2, 0.0)
    return jnp.where(lane == 0, g_idx, comb)


def _outproj_kernel(attn_ref, rec_ref, x_ref, ga_ref, gr_ref, w_ref, g_ref, b_ref, wr_ref, br_ref,
                    x1_ref, gidx_ref, *, alpha, d, n_sub):
    sub = x_ref.shape[0] // n_sub
    rows = [slice(k * sub, (k + 1) * sub) for k in range(n_sub)]
    mixed = [jnp.concatenate([_rms(attn_ref[r, :].astype(F32), ga_ref[...]),
                              _rms(rec_ref[r, :].astype(F32), gr_ref[...])],
                             axis=-1).astype(BF16) for r in rows]
    h = [jnp.dot(m, w_ref[...], preferred_element_type=F32) for m in mixed]
    logits = []
    for r, hk in zip(rows, h):
        x1 = _layer_norm(alpha * x_ref[r, :] + hk, g_ref[...], b_ref[...])
        x1_ref[r, 0:d] = x1
        logits.append(jnp.dot(x1.astype(BF16), wr_ref[...], preferred_element_type=F32) + br_ref[...])
    pick = ((lax.broadcasted_iota(jnp.int32, (SUBLANES, LANES), 0) == 0)
            & (lax.broadcasted_iota(jnp.int32, (SUBLANES, LANES), 1) == 0)).astype(BF16)
    for r, lg in zip(rows, logits):
        route = _route(lg)
        x1_ref[r, d:] = route
        moved = lax.dot_general(pick, route.astype(BF16), (((1,), (1,)), ((), ())), preferred_element_type=F32)
        gidx_ref[0:1, r] = moved[0:1, :]


def _out_proj(attn2, rec2, x2, g_a, g_r, w_out_bf, ln_g, ln_b, w_router_bf, b_router, alpha):
    t, d = x2.shape
    aw, rw = attn2.shape[1], rec2.shape[1]
    tm = min(1024, t)
    n_sub = 2 if tm % 1024 == 0 else 1
    row = lambda w: pl.BlockSpec((tm, w), lambda i: (i, 0))
    full = lambda shape: pl.BlockSpec(shape, lambda i: (0,) * len(shape))
    return pl.pallas_call(
        functools.partial(_outproj_kernel, alpha=alpha, d=d, n_sub=n_sub),
        grid=(t // tm,),
        in_specs=[row(aw), row(rw), row(d), full((1, aw)), full((1, rw)), full((aw + rw, d)),
                  full((1, d)), full((1, d)), full((d, LANES)), full((1, LANES))],
        out_specs=[row(d + LANES), pl.BlockSpec((1, tm), lambda i: (0, i))],
        out_shape=[jax.ShapeDtypeStruct((t, d + LANES), F32), jax.ShapeDtypeStruct((1, t), F32)],
        compiler_params=_cparams(("parallel",)),
        name="out_proj_ln_router",
    )(attn2, rec2, x2, g_a, g_r, w_out_bf, ln_g, ln_b, w_router_bf, b_router)


def _dispatch_plan(gidx, tm):
    t = gidx.shape[0]
    n_tiles = t // tm + N_GROUPS
    counts = jnp.sum((gidx[:, None] == jnp.arange(N_GROUPS)[None, :]).astype(jnp.int32), axis=0)
    tiles_g = (counts + tm - 1) // tm
    tile_end = jnp.cumsum(tiles_g)
    tile_start = tile_end - tiles_g
    n_used = tile_end[-1]
    sorted_start = jnp.cumsum(counts) - counts
    order = jnp.argsort(gidx, stable=True).astype(jnp.int32)
    tile = jnp.arange(n_tiles, dtype=jnp.int32)
    tg = jnp.sum((tile_end[None, :] <= tile[:, None]).astype(jnp.int32), axis=1)
    used = tile < n_used
    tg = jnp.where(used, tg, tg[n_used - 1])
    row_off = (tile - tile_start[tg]) * tm
    nvalid = jnp.where(used, jnp.clip(counts[tg] - row_off, 0, tm), 0).astype(jnp.int32)
    r = jnp.arange(tm, dtype=jnp.int32)[None, :]
    pos = sorted_start[tg][:, None] + row_off[:, None] + jnp.minimum(r, nvalid[:, None] - 1)
    src = order[jnp.clip(pos, 0, t - 1)]
    return src.reshape(-1), tg.astype(jnp.int32), nvalid


def _moe_kernel(src_ref, tg_ref, nv_ref, x_hbm, wg_ref, wu_ref, wd_ref, g_ref, b_ref, out_hbm,
                xbuf0, xbuf1, obuf0, obuf1, gsem, ssem, *, alpha, tm, d, n_tiles):
    i = pl.program_id(0)
    nxt = jnp.minimum(i + 1, n_tiles - 1)
    prv = jnp.maximum(i - 1, 0)
    nv_prv = jnp.where(i > 0, nv_ref[prv], 0)
    prv_full = nv_prv == tm

    def gather_row(tile, r, xbuf, sem):
        tok = src_ref[tile * tm + r]
        return pltpu.make_async_copy(x_hbm.at[pl.ds(tok, 1), :], xbuf.at[pl.ds(r, 1), :], sem)

    def scatter_row(tile, r, obuf, sem):
        tok = src_ref[tile * tm + r]
        return pltpu.make_async_copy(obuf.at[pl.ds(r, 1), :], out_hbm.at[pl.ds(tok, 1), :], sem)

    gather_tile = lambda xbuf, sem: pltpu.make_async_copy(x_hbm.at[pl.ds(0, tm), :], xbuf.at[pl.ds(0, tm), :], sem)
    scatter_tile = lambda obuf, sem: pltpu.make_async_copy(obuf, out_hbm.at[pl.ds(0, tm), :], sem)

    def for_rows(n, fn, unroll):
        def body(r, _):
            fn(r)
            return 0
        lax.fori_loop(0, n, body, 0, unroll=unroll)

    def step(x_cur, x_nxt, o_cur, o_prv, g_cur, g_nxt, s_cur, s_prv):
        @pl.when(i == 0)
        def _prologue():
            for_rows(tm, lambda r: gather_row(0, r, x_cur, g_cur).start(), 8)
            x_cur[tm:, :] = jnp.zeros((SUBLANES, d + LANES), F32)
            x_nxt[tm:, :] = jnp.zeros((SUBLANES, d + LANES), F32)

        @pl.when(prv_full)
        def _scatter_full():
            for r in range(tm):
                scatter_row(prv, r, o_prv, s_prv).start()

        @pl.when(jnp.logical_not(prv_full))
        def _scatter_partial():
            for_rows(nv_prv, lambda r: scatter_row(prv, r, o_prv, s_prv).start(), 1)

        gather_tile(x_cur, g_cur).wait()
        x = x_cur[0:tm, 0:d]
        route = x_cur[0:tm, d:]
        xb = x.astype(BF16)
        lane = lax.broadcasted_iota(jnp.int32, route.shape, 1)
        base = N_GROUPS + EXPERTS_PER_GROUP * tg_ref[i]
        rows_per = tm // EXPERTS_PER_GROUP
        de = wg_ref.shape[2]
        y = jnp.zeros((tm, d), F32)
        for e in range(EXPERTS_PER_GROUP):
            for r in range(e * rows_per, (e + 1) * rows_per):
                gather_row(nxt, r, x_nxt, g_nxt).start()
            zero_row = x_nxt[tm:tm + 1, 0:de]
            gate = jnp.dot(xb, wg_ref[e], preferred_element_type=F32)
            up = jnp.dot(xb, wu_ref[e], preferred_element_type=F32)
            hidden = (jax.nn.silu(gate) * up + zero_row).astype(BF16)
            w_e = jnp.sum(jnp.where(lane == base + e, route, 0.0), axis=-1, keepdims=True)
            y = y + w_e * jnp.dot(hidden, wd_ref[e], preferred_element_type=F32)
        o_cur[...] = _layer_norm(alpha * x + y, g_ref[...], b_ref[...])

        @pl.when(prv_full)
        def _drain_full():
            scatter_tile(o_prv, s_prv).wait()

        @pl.when(jnp.logical_not(prv_full))
        def _drain_partial():
            for_rows(nv_prv, lambda r: scatter_row(prv, r, o_prv, s_prv).wait(), 1)

        @pl.when(i == n_tiles - 1)
        def _epilogue():
            gather_tile(x_nxt, g_nxt).wait()
            for_rows(nv_ref[i], lambda r: scatter_row(i, r, o_cur, s_cur).start(), 1)
            for_rows(nv_ref[i], lambda r: scatter_row(i, r, o_cur, s_cur).wait(), 1)

    @pl.when(i % 2 == 0)
    def _even():
        step(xbuf0, xbuf1, obuf0, obuf1, gsem.at[0], gsem.at[1], ssem.at[0], ssem.at[1])

    @pl.when(i % 2 == 1)
    def _odd():
        step(xbuf1, xbuf0, obuf1, obuf0, gsem.at[1], gsem.at[0], ssem.at[1], ssem.at[0])


def _moe(x1e, gidx, t, wg_bf, wu_bf, wd_bf, ln_g, ln_b, alpha, d):
    n_experts, _, de = wg_bf.shape
    tm = min(512, t)
    src, tg, nvalid = _dispatch_plan(gidx, tm)
    n_tiles = tg.shape[0]
    epg = EXPERTS_PER_GROUP
    full = lambda shape: pl.BlockSpec(shape, lambda i, *_: (0,) * len(shape))
    grid_spec = pltpu.PrefetchScalarGridSpec(
        num_scalar_prefetch=3,
        grid=(n_tiles,),
        in_specs=[pl.BlockSpec(memory_space=pl.ANY),
                  pl.BlockSpec((epg, d, de), lambda i, src_, tg_, nv_: (tg_[i], 0, 0)),
                  pl.BlockSpec((epg, d, de), lambda i, src_, tg_, nv_: (tg_[i], 0, 0)),
                  pl.BlockSpec((epg, de, d), lambda i, src_, tg_, nv_: (tg_[i], 0, 0)),
                  full((1, d)), full((1, d))],
        out_specs=pl.BlockSpec(memory_space=pl.ANY),
        scratch_shapes=[pltpu.VMEM((tm + SUBLANES, d + LANES), F32), pltpu.VMEM((tm + SUBLANES, d + LANES), F32),
                        pltpu.VMEM((tm, d), F32), pltpu.VMEM((tm, d), F32),
                        pltpu.SemaphoreType.DMA((2,)),
                        pltpu.SemaphoreType.DMA((2,))],
    )
    return pl.pallas_call(
        functools.partial(_moe_kernel, alpha=alpha, tm=tm, d=d, n_tiles=n_tiles),
        grid_spec=grid_spec,
        out_shape=jax.ShapeDtypeStruct((t, d), F32),
        compiler_params=_cparams(("arbitrary",)),
        name="moe_ln",
    )(src, tg, nvalid, x1e, wg_bf, wu_bf, wd_bf, ln_g, ln_b)


def kernel(x, w_in, conv_w, conv_b, w_rg_a, b_rg_a, w_rg_i, b_rg_i, lru_lambda, g_attn_norm, g_rec_norm, w_out,
           ln1_g, ln1_b, w_router_group, b_router_group, w_router_expert, b_router_expert, w_gate, w_up, w_down,
           ln2_g, ln2_b):
    b, s, d = x.shape
    depth = w_in.shape[0]
    rw = conv_w.shape[2]
    aw = (w_in.shape[2] - 2 * rw) // 3
    alpha = (2 * depth) ** 0.25
    tables = _rope_tables(s)
    row = lambda v: v.reshape(1, -1)

    x2 = x.reshape(b * s, d)
    for l in range(depth):
        q, k, v, rec = _in_proj(x2, w_in[l].astype(BF16), tables, s, aw)
        attn = _moba_attention(q.reshape(b, s, aw), k.reshape(b, s, aw), v.reshape(b, s, aw))
        w_gates = jnp.concatenate([_block_diag(w_rg_a[l]), _block_diag(w_rg_i[l])], axis=1).astype(BF16)
        rec_out = _rglru(rec.reshape(b, s, 2 * rw), conv_w[l], row(conv_b[l]), w_gates,
                         row(b_rg_a[l]), row(b_rg_i[l]), row(lru_lambda[l]))
        n_route = N_GROUPS + N_GROUPS * EXPERTS_PER_GROUP
        w_router = jnp.concatenate([w_router_group[l], w_router_expert[l]], axis=1)
        w_router = jnp.pad(w_router, ((0, 0), (0, LANES - n_route))).astype(BF16)
        b_router = jnp.pad(jnp.concatenate([b_router_group[l], b_router_expert[l]]), (0, LANES - n_route))
        x1e, gidx = _out_proj(attn.reshape(b * s, aw), rec_out.reshape(b * s, rw), x2,
                        row(g_attn_norm[l]), row(g_rec_norm[l]), w_out[l].astype(BF16),
                        row(ln1_g[l]), row(ln1_b[l]), w_router, row(b_router), alpha)
        x2 = _moe(x1e, gidx[0].astype(jnp.int32), b * s, w_gate[l].astype(BF16), w_up[l].astype(BF16), w_down[l].astype(BF16),
                  row(ln2_g[l]), row(ln2_b[l]), alpha, d)
    return x2.reshape(b, s, d)
```

```python
import functools

import jax
import jax.numpy as jnp
from jax import lax
from jax.experimental import pallas as pl
from jax.experimental.pallas import tpu as pltpu

F32 = jnp.float32
BF16 = jnp.bfloat16

HEAD_DIM = 64
ROT_DIM = HEAD_DIM // 4
ROPE_THETA = 500000.0
MOBA_BLOCK = 256
MOBA_TOPK = 3
CONV_WIDTH = 4
LRU_C = 8.0
N_GROUPS = 4
EXPERTS_PER_GROUP = 4
LN_EPS = 1e-5
RMS_EPS = 1e-6

LANES = 128
SUBLANES = 8
NEG_BIG = -1e30
LOG2E = 1.4426950408889634
Q_SCALE = HEAD_DIM ** -0.5 * LOG2E
AHEAD, BEHIND = 3, 1
VT_ROWS = HEAD_DIM + 16
VMEM_LIMIT = 56 * 1024 * 1024


def _cparams(sem):
    return pltpu.CompilerParams(dimension_semantics=sem, vmem_limit_bytes=VMEM_LIMIT)


def _inproj_kernel(x_ref, w_ref, c_ref, s1_ref, s2_ref, q_ref, k_ref, v_ref, rec_ref, *, aw):
    xb = x_ref[...].astype(BF16)
    cos, sin_lo, sin_hi = c_ref[...], s1_ref[...], s2_ref[...]

    def rope(t):
        outs = []
        for c in range(aw // LANES):
            tc = t[:, c * LANES:(c + 1) * LANES]
            outs.append(tc * cos + pltpu.roll(tc, 8, 1) * sin_lo + pltpu.roll(tc, LANES - 8, 1) * sin_hi)
        return jnp.concatenate(outs, axis=1)

    q = jnp.dot(xb, w_ref[:, 0:aw].astype(BF16), preferred_element_type=F32)
    q_ref[...] = (rope(q) * Q_SCALE).astype(BF16)
    k = jnp.dot(xb, w_ref[:, aw:2 * aw].astype(BF16), preferred_element_type=F32)
    k_ref[...] = rope(k).astype(BF16)
    v = jnp.dot(xb, w_ref[:, 2 * aw:3 * aw].astype(BF16), preferred_element_type=F32)
    v_ref[...] = v.astype(BF16)
    rec_ref[...] = jnp.dot(xb, w_ref[:, 3 * aw:].astype(BF16), preferred_element_type=F32)


def _rope_tables(seq):
    half = ROT_DIM // 2
    inv_freq = ROPE_THETA ** (-jnp.arange(half, dtype=F32) * 2.0 / ROT_DIM)
    ang = jnp.arange(seq).astype(F32)[:, None] * inv_freq[None, :]
    cos, sin = jnp.cos(ang), jnp.sin(ang)
    zeros = jnp.zeros((seq, HEAD_DIM - ROT_DIM), F32)
    zh = jnp.zeros((seq, half), F32)
    c64 = jnp.concatenate([cos, cos, zeros + 1.0], axis=1)
    lo64 = jnp.concatenate([zh, sin, zeros], axis=1)
    hi64 = jnp.concatenate([-sin, zh, zeros], axis=1)
    rep = LANES // HEAD_DIM
    return jnp.tile(c64, (1, rep)), jnp.tile(lo64, (1, rep)), jnp.tile(hi64, (1, rep))


def _in_proj(x2, w_all, layer, tables, seq, aw):
    t, d = x2.shape
    n = w_all.shape[2]
    tm = min(512, seq)
    nseq = seq // tm
    tab_spec = pl.BlockSpec((tm, LANES), lambda i: (i % nseq, 0))
    return pl.pallas_call(
        functools.partial(_inproj_kernel, aw=aw),
        grid=(t // tm,),
        in_specs=[pl.BlockSpec((tm, d), lambda i: (i, 0)),
                  pl.BlockSpec((None, d, n), lambda i: (layer, 0, 0)),
                  tab_spec, tab_spec, tab_spec],
        out_specs=[pl.BlockSpec((tm, aw), lambda i: (i, 0)),
                   pl.BlockSpec((tm, aw), lambda i: (i, 0)),
                   pl.BlockSpec((tm, aw), lambda i: (i, 0)),
                   pl.BlockSpec((tm, n - 3 * aw), lambda i: (i, 0))],
        out_shape=[jax.ShapeDtypeStruct((t, aw), BF16),
                   jax.ShapeDtypeStruct((t, aw), BF16),
                   jax.ShapeDtypeStruct((t, aw), BF16),
                   jax.ShapeDtypeStruct((t, n - 3 * aw), F32)],
        compiler_params=_cparams(("parallel",)),
        name="in_proj",
    )(x2, w_all, *tables)


def _attn_kernel(q_ref, k_ref, v_ref, o_ref, vt_ref, km_ref, bias_ref, m_ref, acc_ref, *, nb, npair):
    blk = MOBA_BLOCK
    i = pl.program_id(1)
    lane = lax.broadcasted_iota(jnp.int32, (1, LANES), 1)
    head_lanes = (lane < HEAD_DIM, lane >= HEAD_DIM)
    nt = (((1,), (1,)), ((), ()))
    heads = [(p, h) for p in range(npair) for h in range(2)]
    pair_lanes = lambda p: slice(p * LANES, (p + 1) * LANES)

    @pl.when(i == 0)
    def _prepare():
        def prep(j, _):
            off = pl.multiple_of(j * blk, blk)
            for p in range(npair):
                kb = k_ref[0, pl.ds(off, blk), pair_lanes(p)].astype(F32)
                km = jnp.sum(kb, axis=0, keepdims=True) * (1.0 / blk)
                km_ref[2 * p, pl.ds(j, 1), :] = jnp.where(head_lanes[0], km, 0.0)
                km_ref[2 * p + 1, pl.ds(j, 1), :] = jnp.where(head_lanes[1], km, 0.0)
                vt = v_ref[0, pl.ds(off, blk), pair_lanes(p)].astype(F32).T
                vt_ref[2 * p, 0:HEAD_DIM, pl.ds(off, blk)] = vt[0:HEAD_DIM].astype(BF16)
                vt_ref[2 * p + 1, 0:HEAD_DIM, pl.ds(off, blk)] = vt[HEAD_DIM:].astype(BF16)
            return 0

        lax.fori_loop(0, nb, prep, 0)
        vt_ref[:, HEAD_DIM:, :] = jnp.ones((2 * npair, VT_ROWS - HEAD_DIM, nb * blk), BF16)

    qh = []
    for p, h in heads:
        q_pair = q_ref[0, :, pair_lanes(p)]
        qh.append(jnp.where(head_lanes[h], q_pair, jnp.zeros_like(q_pair)))

    def block_gate(n):
        p, h = heads[n]
        n_iota = lax.broadcasted_iota(jnp.int32, (nb, blk), 0)
        valid = n_iota < i
        g = lax.dot_general(km_ref[n], q_ref[0, :, pair_lanes(p)].astype(F32), nt, preferred_element_type=F32)
        g = jnp.where(valid, g, -jnp.inf)
        bias = jnp.full((nb, blk), NEG_BIG, F32)
        for _ in range(min(MOBA_TOPK, nb)):
            top = jnp.max(g, axis=0, keepdims=True)
            first = jnp.min(jnp.where(g == top, n_iota, nb), axis=0, keepdims=True)
            hit = n_iota == first
            bias = jnp.where(hit & valid, 0.0, bias)
            g = jnp.where(hit, -jnp.inf, g)
        bias_ref[n] = bias

    off_i = pl.multiple_of(i * blk, blk)
    kpos = lax.broadcasted_iota(jnp.int32, (blk, blk), 0)
    qpos = lax.broadcasted_iota(jnp.int32, (blk, blk), 1)
    causal = kpos <= qpos

    def scores(n, off, rows):
        kb = k_ref[0, pl.ds(off, rows), pair_lanes(heads[n][0])]
        return lax.dot_general(kb, qh[n], nt, preferred_element_type=F32)

    def interleave(stage_scores, stage_probs, stage_values):
        nh = len(heads)
        s_q = {n: stage_scores(n) for n in range(min(AHEAD, nh))}
        p_q = {}
        for t in range(nh + BEHIND):
            if t + AHEAD < nh:
                s_q[t + AHEAD] = stage_scores(t + AHEAD)
            if t < nh:
                p_q[t] = stage_probs(t, s_q.pop(t))
            if t >= BEHIND:
                stage_values(t - BEHIND, *p_q.pop(t - BEHIND))

    def own_probs(n, s):
        s = jnp.where(causal, s, NEG_BIG)
        m0 = jnp.max(s, axis=0, keepdims=True)
        m_ref[n] = m0
        return (jnp.exp2(s - m0).astype(BF16),)

    def own_values(n, pr):
        acc_ref[n] = jnp.dot(vt_ref[n, :, pl.ds(off_i, blk)], pr, preferred_element_type=F32)
        block_gate(n)

    interleave(lambda n: scores(n, off_i, blk), own_probs, own_values)

    def body(jj, _):
        j0 = 2 * jj
        off = pl.multiple_of(j0 * blk, 2 * blk)

        def probs(n, s):
            b0 = bias_ref[n, pl.ds(j0, 1), :]
            b1 = bias_ref[n, pl.ds(j0 + 1, 1), :]
            s0, s1 = s[0:blk], s[blk:]
            mj = jnp.maximum(jnp.max(s0, axis=0, keepdims=True) + b0, jnp.max(s1, axis=0, keepdims=True) + b1)
            m_run = m_ref[n]
            m_new = jnp.maximum(m_run, mj)
            m_ref[n] = m_new
            alpha = jnp.exp2(m_run - m_new)
            p0 = jnp.exp2(s0 - (m_new - b0)).astype(BF16)
            p1 = jnp.exp2(s1 - (m_new - b1)).astype(BF16)
            return alpha, p0, p1

        def accumulate(n, alpha, p0, p1):
            pv = (jnp.dot(vt_ref[n, :, pl.ds(off, blk)], p0, preferred_element_type=F32)
                  + jnp.dot(vt_ref[n, :, pl.ds(off + blk, blk)], p1, preferred_element_type=F32))
            acc_ref[n] = alpha * acc_ref[n] + pv

        interleave(lambda n: scores(n, off, 2 * blk), probs, accumulate)
        return 0

    lax.fori_loop(0, (i + 1) // 2, body, 0)
    for p in range(npair):
        outs = []
        for h in range(2):
            acc = acc_ref[2 * p + h]
            outs.append(acc[0:HEAD_DIM] / acc[HEAD_DIM:HEAD_DIM + 1])
        o_ref[0, :, pair_lanes(p)] = jnp.concatenate(outs, axis=0).T.astype(o_ref.dtype)


def _moba_attention(q, k, v):
    b, s, aw = q.shape
    nb = s // MOBA_BLOCK
    npair = aw // LANES
    nh = 2 * npair
    return pl.pallas_call(
        functools.partial(_attn_kernel, nb=nb, npair=npair),
        grid=(b, nb),
        in_specs=[pl.BlockSpec((1, MOBA_BLOCK, aw), lambda b_, i: (b_, i, 0)),
                  pl.BlockSpec((1, s, aw), lambda b_, i: (b_, 0, 0)),
                  pl.BlockSpec((1, s, aw), lambda b_, i: (b_, 0, 0))],
        out_specs=pl.BlockSpec((1, MOBA_BLOCK, aw), lambda b_, i: (b_, i, 0)),
        out_shape=jax.ShapeDtypeStruct((b, s, aw), BF16),
        scratch_shapes=[pltpu.VMEM((nh, VT_ROWS, s), BF16),
                        pltpu.VMEM((nh, nb, LANES), F32),
                        pltpu.VMEM((nh, nb, MOBA_BLOCK), F32),
                        pltpu.VMEM((nh, 1, MOBA_BLOCK), F32),
                        pltpu.VMEM((nh, VT_ROWS, MOBA_BLOCK), F32)],
        compiler_params=_cparams(("parallel", "arbitrary")),
        name="moba_attn",
    )(q, k, v)


def _rglru_kernel(rec_ref, cw_ref, cb_ref, wg_ref, ba_ref, bi_ref, lam_ref, o_ref,
                  tail_ref, h_ref, a_ref, u_ref, hin_ref, xs_ref, *, rw):
    ts = rec_ref.shape[1]
    t_idx = pl.program_id(1)

    @pl.when(t_idx == 0)
    def _reset():
        tail_ref[...] = jnp.zeros_like(tail_ref)
        h_ref[...] = jnp.zeros_like(h_ref)

    xr = rec_ref[0, :, 0:rw]
    xs_ref[0:SUBLANES, :] = tail_ref[...]
    xs_ref[SUBLANES:, :] = xr
    tail_ref[...] = xr[ts - SUBLANES:, :]
    xc = xr * cw_ref[CONV_WIDTH - 1:CONV_WIDTH, :] + cb_ref[...]
    for d in range(1, CONV_WIDTH):
        xc = xc + xs_ref[SUBLANES - d:SUBLANES - d + ts, :] * cw_ref[CONV_WIDTH - 1 - d:CONV_WIDTH - d, :]

    pre = jnp.dot(xc.astype(BF16), wg_ref[...], preferred_element_type=F32)
    r = jax.nn.sigmoid(pre[:, 0:rw] + ba_ref[...])
    gi = jax.nn.sigmoid(pre[:, rw:] + bi_ref[...])
    lam = lam_ref[...]
    softplus_neg = jnp.maximum(-lam, 0.0) + jnp.log1p(jnp.exp(-jnp.abs(lam)))
    log_a = (-LRU_C * r) * softplus_neg
    a = jnp.exp(log_a)
    z = -jnp.tanh(log_a) * (a * a + 1.0)
    root = jnp.where(z > 0.0, z * lax.rsqrt(z), 0.0)
    u = root * (gi * xc)

    ng = ts // SUBLANES
    sub = lax.broadcasted_iota(jnp.int32, (ts, 1), 0) % SUBLANES
    d = 1
    while d < SUBLANES:
        live = sub >= d
        u = jnp.where(live, a * pltpu.roll(u, d, 0) + u, u)
        a = jnp.where(live, a * pltpu.roll(a, d, 0), a)
        d *= 2
    nplane = rw // LANES
    last_rows = pl.ds(SUBLANES - 1, ng, stride=SUBLANES)
    for c in range(nplane):
        a_ref[c] = a[:, c * LANES:(c + 1) * LANES]
        u_ref[c] = u[:, c * LANES:(c + 1) * LANES]
    ag = jnp.concatenate([a_ref[c, last_rows, :] for c in range(nplane)], axis=1)
    ug = jnp.concatenate([u_ref[c, last_rows, :] for c in range(nplane)], axis=1)
    grow = lax.broadcasted_iota(jnp.int32, (ng, 1), 0)
    d = 1
    while d < ng:
        live = grow >= d
        ug = jnp.where(live, ag * pltpu.roll(ug, d, 0) + ug, ug)
        ag = jnp.where(live, ag * pltpu.roll(ag, d, 0), ag)
        d *= 2
    h0 = h_ref[...]
    hg = ag * h0 + ug
    hin_ref[...] = jnp.where(grow >= 1, pltpu.roll(hg, 1, 0), h0)
    h_ref[...] = hg[ng - 1:, :]
    for g2 in range(0, ng, 2):
        hs = []
        for g in (g2, g2 + 1):
            rows = slice(g * SUBLANES, (g + 1) * SUBLANES)
            a_g = jnp.concatenate([a_ref[c, rows, :] for c in range(nplane)], axis=1)
            u_g = jnp.concatenate([u_ref[c, rows, :] for c in range(nplane)], axis=1)
            hs.append(a_g * hin_ref[g:g + 1, :] + u_g)
        rows = slice(g2 * SUBLANES, (g2 + 2) * SUBLANES)
        gated = jnp.concatenate(hs, axis=0) * jax.nn.gelu(rec_ref[0, rows, rw:])
        o_ref[0, rows, :] = gated.astype(o_ref.dtype)


def _rglru(rec, conv_w, conv_b, w_gates_bf, b_a, b_i, lam):
    b, s, two_rw = rec.shape
    rw = two_rw // 2
    ts = min(512, s)
    full = lambda shape: pl.BlockSpec(shape, lambda b_, t: (0,) * len(shape))
    return pl.pallas_call(
        functools.partial(_rglru_kernel, rw=rw),
        grid=(b, s // ts),
        in_specs=[pl.BlockSpec((1, ts, two_rw), lambda b_, t: (b_, t, 0)),
                  full((CONV_WIDTH, rw)), full((1, rw)), full((rw, two_rw)),
                  full((1, rw)), full((1, rw)), full((1, rw))],
        out_specs=pl.BlockSpec((1, ts, rw), lambda b_, t: (b_, t, 0)),
        out_shape=jax.ShapeDtypeStruct((b, s, rw), BF16),
        scratch_shapes=[pltpu.VMEM((SUBLANES, rw), F32), pltpu.VMEM((1, rw), F32),
                        pltpu.VMEM((rw // LANES, ts, LANES), F32), pltpu.VMEM((rw // LANES, ts, LANES), F32),
                        pltpu.VMEM((ts // SUBLANES, rw), F32), pltpu.VMEM((ts + SUBLANES, rw), F32)],
        compiler_params=_cparams(("parallel", "arbitrary")),
        name="rglru",
    )(rec, conv_w, conv_b, w_gates_bf, b_a, b_i, lam)


def _block_diag(w):
    n, c, d = w.shape
    eye = jnp.eye(n, dtype=w.dtype)
    return (w[:, :, None, :] * eye[:, None, :, None]).reshape(n * c, n * d)


def _layer_norm(z, g, b):
    mu = jnp.mean(z, axis=-1, keepdims=True)
    zc = z - mu
    var = jnp.mean(zc * zc, axis=-1, keepdims=True)
    return zc * lax.rsqrt(var + LN_EPS) * g + b


def _rms(z, g):
    return z * lax.rsqrt(jnp.mean(z * z, axis=-1, keepdims=True) + RMS_EPS) * g


def _route(logits):
    lane = lax.broadcasted_iota(jnp.int32, logits.shape, 1).astype(F32)
    neg_inf = -jnp.inf
    first_lane = lambda mask: jnp.min(jnp.where(mask, lane, float(LANES)), axis=-1, keepdims=True)
    gl = jnp.where(lane < N_GROUPS, logits, neg_inf)
    gmax = jnp.max(gl, axis=-1, keepdims=True)
    gsum = jnp.sum(jnp.exp(gl - gmax), axis=-1, keepdims=True)
    g_p = 1.0 / gsum
    g_idx = first_lane(gl == gmax)
    lo = N_GROUPS + EXPERTS_PER_GROUP * g_idx
    emask = (lane >= lo) & (lane < lo + EXPERTS_PER_GROUP)
    el = jnp.where(emask, logits, neg_inf)
    emax = jnp.max(el, axis=-1, keepdims=True)
    ee = jnp.exp(el - emax)
    ep = ee / jnp.sum(ee, axis=-1, keepdims=True)
    e1 = jnp.max(ep, axis=-1, keepdims=True)
    i1 = first_lane(emask & (ep == e1))
    rest = emask & (lane != i1)
    ep2 = jnp.where(rest, ep, neg_inf)
    e2 = jnp.max(ep2, axis=-1, keepdims=True)
    i2 = first_lane(rest & (ep2 == e2))
    denom = e1 + e2
    w1 = g_p * e1 / denom
    w2 = g_p * e2 / denom
    comb = jnp.where(lane == i1, w1, 0.0) + jnp.where(lane == i2, w2, 0.0)
    return jnp.where(lane == 0, g_idx, comb)


def _outproj_kernel(attn_ref, rec_ref, x_ref, ga_ref, gr_ref, w_ref, g_ref, b_ref, wr_ref, br_ref,
                    x1_ref, gidx_ref, *, alpha, d, n_sub):
    sub = x_ref.shape[0] // n_sub
    rows = [slice(k * sub, (k + 1) * sub) for k in range(n_sub)]
    mixed = [jnp.concatenate([_rms(attn_ref[r, :].astype(F32), ga_ref[...]),
                              _rms(rec_ref[r, :].astype(F32), gr_ref[...])],
                             axis=-1).astype(BF16) for r in rows]
    w = w_ref[...].astype(BF16)
    h = [jnp.dot(m, w, preferred_element_type=F32) for m in mixed]
    logits = []
    for r, hk in zip(rows, h):
        x1 = _layer_norm(alpha * x_ref[r, :] + hk, g_ref[...], b_ref[...])
        x1_ref[r, 0:d] = x1
        logits.append(jnp.dot(x1.astype(BF16), wr_ref[...], preferred_element_type=F32) + br_ref[...])
    pick = ((lax.broadcasted_iota(jnp.int32, (SUBLANES, LANES), 0) == 0)
            & (lax.broadcasted_iota(jnp.int32, (SUBLANES, LANES), 1) == 0)).astype(BF16)
    for r, lg in zip(rows, logits):
        route = _route(lg)
        x1_ref[r, d:] = route
        moved = lax.dot_general(pick, route.astype(BF16), (((1,), (1,)), ((), ())), preferred_element_type=F32)
        gidx_ref[0:1, r] = moved[0:1, :]


def _out_proj(attn2, rec2, x2, g_a, g_r, w_out_all, layer, ln_g, ln_b, w_router_bf, b_router, alpha):
    t, d = x2.shape
    aw, rw = attn2.shape[1], rec2.shape[1]
    tm = min(1024, t)
    n_sub = 2 if tm % 1024 == 0 else 1
    row = lambda w: pl.BlockSpec((tm, w), lambda i: (i, 0))
    full = lambda shape: pl.BlockSpec(shape, lambda i: (0,) * len(shape))
    return pl.pallas_call(
        functools.partial(_outproj_kernel, alpha=alpha, d=d, n_sub=n_sub),
        grid=(t // tm,),
        in_specs=[row(aw), row(rw), row(d), full((1, aw)), full((1, rw)),
                  pl.BlockSpec((None, aw + rw, d), lambda i: (layer, 0, 0)),
                  full((1, d)), full((1, d)), full((d, LANES)), full((1, LANES))],
        out_specs=[row(d + LANES), pl.BlockSpec((1, tm), lambda i: (0, i))],
        out_shape=[jax.ShapeDtypeStruct((t, d + LANES), F32), jax.ShapeDtypeStruct((1, t), F32)],
        compiler_params=_cparams(("parallel",)),
        name="out_proj_ln_router",
    )(attn2, rec2, x2, g_a, g_r, w_out_all, ln_g, ln_b, w_router_bf, b_router)


def _dispatch_plan(gidx, tm):
    t = gidx.shape[0]
    n_tiles = t // tm + N_GROUPS
    counts = jnp.sum((gidx[:, None] == jnp.arange(N_GROUPS)[None, :]).astype(jnp.int32), axis=0)
    tiles_g = (counts + tm - 1) // tm
    tile_end = jnp.cumsum(tiles_g)
    tile_start = tile_end - tiles_g
    n_used = tile_end[-1]
    sorted_start = jnp.cumsum(counts) - counts
    order = jnp.argsort(gidx, stable=True).astype(jnp.int32)
    tile = jnp.arange(n_tiles, dtype=jnp.int32)
    tg = jnp.sum((tile_end[None, :] <= tile[:, None]).astype(jnp.int32), axis=1)
    used = tile < n_used
    tg = jnp.where(used, tg, tg[n_used - 1])
    row_off = (tile - tile_start[tg]) * tm
    nvalid = jnp.where(used, jnp.clip(counts[tg] - row_off, 0, tm), 0).astype(jnp.int32)
    r = jnp.arange(tm, dtype=jnp.int32)[None, :]
    pos = sorted_start[tg][:, None] + row_off[:, None] + jnp.minimum(r, nvalid[:, None] - 1)
    src = order[jnp.clip(pos, 0, t - 1)]
    return src.reshape(-1), tg.astype(jnp.int32), nvalid


def _moe_kernel(src_ref, tg_ref, nv_ref, x_hbm, wg_ref, wu_ref, wd_ref, g_ref, b_ref, out_hbm,
                xbuf0, xbuf1, obuf0, obuf1, gsem, ssem, *, alpha, tm, d, n_tiles):
    i = pl.program_id(0)
    nxt = jnp.minimum(i + 1, n_tiles - 1)
    prv = jnp.maximum(i - 1, 0)
    nv_prv = jnp.where(i > 0, nv_ref[prv], 0)
    pp = jnp.maximum(i - 2, 0)
    nv_pp = jnp.where(i > 1, nv_ref[pp], 0)
    prv_full = nv_prv == tm

    def gather_row(tile, r, xbuf, sem):
        tok = src_ref[tile * tm + r]
        return pltpu.make_async_copy(x_hbm.at[pl.ds(tok, 1), :], xbuf.at[pl.ds(r, 1), :], sem)

    def scatter_row(tile, r, obuf, sem):
        tok = src_ref[tile * tm + r]
        return pltpu.make_async_copy(obuf.at[pl.ds(r, 1), :], out_hbm.at[pl.ds(tok, 1), :], sem)

    gather_tile = lambda xbuf, sem: pltpu.make_async_copy(x_hbm.at[pl.ds(0, tm), :], xbuf.at[pl.ds(0, tm), :], sem)
    scatter_tile = lambda obuf, sem: pltpu.make_async_copy(obuf, out_hbm.at[pl.ds(0, tm), :], sem)

    def for_rows(n, fn, unroll):
        def body(r, _):
            fn(r)
            return 0
        lax.fori_loop(0, n, body, 0, unroll=unroll)

    def step(x_cur, x_nxt, o_cur, o_prv, g_cur, g_nxt, s_cur, s_prv):
        @pl.when(i == 0)
        def _prologue():
            for_rows(tm, lambda r: gather_row(0, r, x_cur, g_cur).start(), 8)
            x_cur[tm:, :] = jnp.zeros((SUBLANES, d + LANES), F32)
            x_nxt[tm:, :] = jnp.zeros((SUBLANES, d + LANES), F32)

        gather_tile(x_cur, g_cur).wait()
        x = x_cur[0:tm, 0:d]
        route = x_cur[0:tm, d:]
        xb = x.astype(BF16)
        lane = lax.broadcasted_iota(jnp.int32, route.shape, 1)
        base = N_GROUPS + EXPERTS_PER_GROUP * tg_ref[i]
        rows_per = tm // EXPERTS_PER_GROUP
        de = wg_ref.shape[2]
        y = jnp.zeros((tm, d), F32)
        for e in range(EXPERTS_PER_GROUP):
            for r in range(e * rows_per, (e + 1) * rows_per):
                gather_row(nxt, r, x_nxt, g_nxt).start()
            zero_row = x_nxt[tm:tm + 1, 0:de]
            gate = jnp.dot(xb, wg_ref[e], preferred_element_type=F32)
            up = jnp.dot(xb, wu_ref[e], preferred_element_type=F32)
            hidden = (jax.nn.silu(gate) * up + zero_row).astype(BF16)
            w_e = jnp.sum(jnp.where(lane == base + e, route, 0.0), axis=-1, keepdims=True)
            y = y + w_e * jnp.dot(hidden, wd_ref[e], preferred_element_type=F32)
        z = alpha * x + y

        @pl.when(nv_pp == tm)
        def _drain_full():
            scatter_tile(o_cur, s_cur).wait()

        @pl.when(nv_pp < tm)
        def _drain_partial():
            for_rows(nv_pp, lambda r: scatter_row(pp, r, o_cur, s_cur).wait(), 1)

        @pl.when(prv_full)
        def _finish_after_full_tile():
            o_cur[...] = _layer_norm(z, g_ref[...], b_ref[...])
            for r in range(tm):
                scatter_row(prv, r, o_prv, s_prv).start()

        @pl.when(jnp.logical_not(prv_full))
        def _finish_after_partial_tile():
            o_cur[...] = _layer_norm(z, g_ref[...], b_ref[...])
            for_rows(nv_prv, lambda r: scatter_row(prv, r, o_prv, s_prv).start(), 1)

        @pl.when(i == n_tiles - 1)
        def _epilogue():
            gather_tile(x_nxt, g_nxt).wait()
            for_rows(nv_prv, lambda r: scatter_row(prv, r, o_prv, s_prv).wait(), 1)
            for_rows(nv_ref[i], lambda r: scatter_row(i, r, o_cur, s_cur).start(), 1)
            for_rows(nv_ref[i], lambda r: scatter_row(i, r, o_cur, s_cur).wait(), 1)

    @pl.when(i % 2 == 0)
    def _even():
        step(xbuf0, xbuf1, obuf0, obuf1, gsem.at[0], gsem.at[1], ssem.at[0], ssem.at[1])

    @pl.when(i % 2 == 1)
    def _odd():
        step(xbuf1, xbuf0, obuf1, obuf0, gsem.at[1], gsem.at[0], ssem.at[1], ssem.at[0])


def _moe(x1e, gidx, t, wg_bf, wu_bf, wd_bf, layer, ln_g, ln_b, alpha, d):
    de = wg_bf.shape[3]
    tm = min(512, t)
    src, tg, nvalid = _dispatch_plan(gidx, tm)
    n_tiles = tg.shape[0]
    epg = EXPERTS_PER_GROUP
    full = lambda shape: pl.BlockSpec(shape, lambda i, *_: (0,) * len(shape))
    grid_spec = pltpu.PrefetchScalarGridSpec(
        num_scalar_prefetch=3,
        grid=(n_tiles,),
        in_specs=[pl.BlockSpec(memory_space=pl.ANY),
                  pl.BlockSpec((None, epg, d, de), lambda i, src_, tg_, nv_: (layer, tg_[i], 0, 0)),
                  pl.BlockSpec((None, epg, d, de), lambda i, src_, tg_, nv_: (layer, tg_[i], 0, 0)),
                  pl.BlockSpec((None, epg, de, d), lambda i, src_, tg_, nv_: (layer, tg_[i], 0, 0)),
                  full((1, d)), full((1, d))],
        out_specs=pl.BlockSpec(memory_space=pl.ANY),
        scratch_shapes=[pltpu.VMEM((tm + SUBLANES, d + LANES), F32), pltpu.VMEM((tm + SUBLANES, d + LANES), F32),
                        pltpu.VMEM((tm, d), F32), pltpu.VMEM((tm, d), F32),
                        pltpu.SemaphoreType.DMA((2,)),
                        pltpu.SemaphoreType.DMA((2,))],
    )
    return pl.pallas_call(
        functools.partial(_moe_kernel, alpha=alpha, tm=tm, d=d, n_tiles=n_tiles),
        grid_spec=grid_spec,
        out_shape=jax.ShapeDtypeStruct((t, d), F32),
        compiler_params=_cparams(("arbitrary",)),
        name="moe_ln",
    )(src, tg, nvalid, x1e, wg_bf, wu_bf, wd_bf, ln_g, ln_b)


def kernel(x, w_in, conv_w, conv_b, w_rg_a, b_rg_a, w_rg_i, b_rg_i, lru_lambda, g_attn_norm, g_rec_norm, w_out,
           ln1_g, ln1_b, w_router_group, b_router_group, w_router_expert, b_router_expert, w_gate, w_up, w_down,
           ln2_g, ln2_b):
    b, s, d = x.shape
    depth = w_in.shape[0]
    rw = conv_w.shape[2]
    aw = (w_in.shape[2] - 2 * rw) // 3
    alpha = (2 * depth) ** 0.25
    tables = _rope_tables(s)
    row = lambda v: v.reshape(1, -1)

    wg_bf, wu_bf, wd_bf = w_gate.astype(BF16), w_up.astype(BF16), w_down.astype(BF16)
    x2 = x.reshape(b * s, d)
    for l in range(depth):
        q, k, v, rec = _in_proj(x2, w_in, l, tables, s, aw)
        attn = _moba_attention(q.reshape(b, s, aw), k.reshape(b, s, aw), v.reshape(b, s, aw))
        w_gates = jnp.concatenate([_block_diag(w_rg_a[l]), _block_diag(w_rg_i[l])], axis=1).astype(BF16)
        rec_out = _rglru(rec.reshape(b, s, 2 * rw), conv_w[l], row(conv_b[l]), w_gates,
                         row(b_rg_a[l]), row(b_rg_i[l]), row(lru_lambda[l]))
        n_route = N_GROUPS + N_GROUPS * EXPERTS_PER_GROUP
        w_router = jnp.concatenate([w_router_group[l], w_router_expert[l]], axis=1)
        w_router = jnp.pad(w_router, ((0, 0), (0, LANES - n_route))).astype(BF16)
        b_router = jnp.pad(jnp.concatenate([b_router_group[l], b_router_expert[l]]), (0, LANES - n_route))
        x1e, gidx = _out_proj(attn.reshape(b * s, aw), rec_out.reshape(b * s, rw), x2,
                        row(g_attn_norm[l]), row(g_rec_norm[l]), w_out, l,
                        row(ln1_g[l]), row(ln1_b[l]), w_router, row(b_router), alpha)
        x2 = _moe(x1e, gidx[0].astype(jnp.int32), b * s, wg_bf, wu_bf, wd_bf, l,
                  row(ln2_g[l]), row(ln2_b[l]), alpha, d)
    return x2.reshape(b, s, d)
```

```python
import functools

import jax
import jax.numpy as jnp
from jax import lax
from jax.experimental import pallas as pl
from jax.experimental.pallas import tpu as pltpu

F32 = jnp.float32
BF16 = jnp.bfloat16

HEAD_DIM = 64
ROT_DIM = HEAD_DIM // 4
ROPE_THETA = 500000.0
MOBA_BLOCK = 256
MOBA_TOPK = 3
CONV_WIDTH = 4
LRU_C = 8.0
N_GROUPS = 4
EXPERTS_PER_GROUP = 4
LN_EPS = 1e-5
RMS_EPS = 1e-6

LANES = 128
SUBLANES = 8
NEG_BIG = -1e30
LOG2E = 1.4426950408889634
Q_SCALE = HEAD_DIM ** -0.5 * LOG2E
AHEAD, BEHIND = 3, 1
VT_ROWS = HEAD_DIM + 16
VMEM_LIMIT = 56 * 1024 * 1024
ROW_TILE = 512
OUT_TILE = 1024


def _cparams(sem):
    return pltpu.CompilerParams(dimension_semantics=sem, vmem_limit_bytes=VMEM_LIMIT)


def _inproj_kernel(x_ref, w_ref, c_ref, s1_ref, s2_ref, q_ref, k_ref, v_ref, rec_ref, *, aw):
    xb = x_ref[...].astype(BF16)
    cos, sin_lo, sin_hi = c_ref[...], s1_ref[...], s2_ref[...]

    def rope(t):
        outs = []
        for c in range(aw // LANES):
            tc = t[:, c * LANES:(c + 1) * LANES]
            outs.append(tc * cos + pltpu.roll(tc, 8, 1) * sin_lo + pltpu.roll(tc, LANES - 8, 1) * sin_hi)
        return jnp.concatenate(outs, axis=1)

    q = jnp.dot(xb, w_ref[:, 0:aw].astype(BF16), preferred_element_type=F32)
    q_ref[...] = (rope(q) * Q_SCALE).astype(BF16)
    k = jnp.dot(xb, w_ref[:, aw:2 * aw].astype(BF16), preferred_element_type=F32)
    k_ref[...] = rope(k).astype(BF16)
    v = jnp.dot(xb, w_ref[:, 2 * aw:3 * aw].astype(BF16), preferred_element_type=F32)
    v_ref[...] = v.astype(BF16)
    rec_ref[...] = jnp.dot(xb, w_ref[:, 3 * aw:].astype(BF16), preferred_element_type=F32)


def _rope_tables(seq):
    half = ROT_DIM // 2
    inv_freq = ROPE_THETA ** (-jnp.arange(half, dtype=F32) * 2.0 / ROT_DIM)
    ang = jnp.arange(seq).astype(F32)[:, None] * inv_freq[None, :]
    cos, sin = jnp.cos(ang), jnp.sin(ang)
    zeros = jnp.zeros((seq, HEAD_DIM - ROT_DIM), F32)
    zh = jnp.zeros((seq, half), F32)
    c64 = jnp.concatenate([cos, cos, zeros + 1.0], axis=1)
    lo64 = jnp.concatenate([zh, sin, zeros], axis=1)
    hi64 = jnp.concatenate([-sin, zh, zeros], axis=1)
    rep = LANES // HEAD_DIM
    return jnp.tile(c64, (1, rep)), jnp.tile(lo64, (1, rep)), jnp.tile(hi64, (1, rep))


def _in_proj(x2, w_all, layer, tables, seq, aw):
    t, d = x2.shape
    n = w_all.shape[2]
    tm = min(ROW_TILE, seq)
    nseq = seq // tm
    tab_spec = pl.BlockSpec((tm, LANES), lambda i: (i % nseq, 0))
    return pl.pallas_call(
        functools.partial(_inproj_kernel, aw=aw),
        grid=(t // tm,),
        in_specs=[pl.BlockSpec((tm, d), lambda i: (i, 0)),
                  pl.BlockSpec((None, d, n), lambda i: (layer, 0, 0)),
                  tab_spec, tab_spec, tab_spec],
        out_specs=[pl.BlockSpec((tm, aw), lambda i: (i, 0)),
                   pl.BlockSpec((tm, aw), lambda i: (i, 0)),
                   pl.BlockSpec((tm, aw), lambda i: (i, 0)),
                   pl.BlockSpec((tm, n - 3 * aw), lambda i: (i, 0))],
        out_shape=[jax.ShapeDtypeStruct((t, aw), BF16),
                   jax.ShapeDtypeStruct((t, aw), BF16),
                   jax.ShapeDtypeStruct((t, aw), BF16),
                   jax.ShapeDtypeStruct((t, n - 3 * aw), F32)],
        compiler_params=_cparams(("parallel",)),
        name="in_proj",
    )(x2, w_all, *tables)


def _attn_kernel(q_ref, k_ref, v_ref, o_ref, vt_ref, km_ref, bias_ref, m_ref, acc_ref, *, nb, npair):
    blk = MOBA_BLOCK
    i = pl.program_id(1)
    lane = lax.broadcasted_iota(jnp.int32, (1, LANES), 1)
    head_lanes = (lane < HEAD_DIM, lane >= HEAD_DIM)
    nt = (((1,), (1,)), ((), ()))
    heads = [(p, h) for p in range(npair) for h in range(2)]
    pair_lanes = lambda p: slice(p * LANES, (p + 1) * LANES)

    @pl.when(i == 0)
    def _prepare():
        def prep(j, _):
            off = pl.multiple_of(j * blk, blk)
            for p in range(npair):
                kb = k_ref[0, pl.ds(off, blk), pair_lanes(p)].astype(F32)
                km = jnp.sum(kb, axis=0, keepdims=True) * (1.0 / blk)
                km_ref[2 * p, pl.ds(j, 1), :] = jnp.where(head_lanes[0], km, 0.0)
                km_ref[2 * p + 1, pl.ds(j, 1), :] = jnp.where(head_lanes[1], km, 0.0)
                vt = v_ref[0, pl.ds(off, blk), pair_lanes(p)].astype(F32).T
                vt_ref[2 * p, 0:HEAD_DIM, pl.ds(off, blk)] = vt[0:HEAD_DIM].astype(BF16)
                vt_ref[2 * p + 1, 0:HEAD_DIM, pl.ds(off, blk)] = vt[HEAD_DIM:].astype(BF16)
            return 0

        lax.fori_loop(0, nb, prep, 0)
        vt_ref[:, HEAD_DIM:, :] = jnp.ones((2 * npair, VT_ROWS - HEAD_DIM, nb * blk), BF16)

    qh = []
    for p, h in heads:
        q_pair = q_ref[0, :, pair_lanes(p)]
        qh.append(jnp.where(head_lanes[h], q_pair, jnp.zeros_like(q_pair)))

    def block_gate(n):
        p, h = heads[n]
        n_iota = lax.broadcasted_iota(jnp.int32, (nb, blk), 0)
        valid = n_iota < i
        g = lax.dot_general(km_ref[n], q_ref[0, :, pair_lanes(p)].astype(F32), nt, preferred_element_type=F32)
        g = jnp.where(valid, g, -jnp.inf)
        bias = jnp.full((nb, blk), NEG_BIG, F32)
        for _ in range(min(MOBA_TOPK, nb)):
            top = jnp.max(g, axis=0, keepdims=True)
            first = jnp.min(jnp.where(g == top, n_iota, nb), axis=0, keepdims=True)
            hit = n_iota == first
            bias = jnp.where(hit & valid, 0.0, bias)
            g = jnp.where(hit, -jnp.inf, g)
        bias_ref[n] = bias

    off_i = pl.multiple_of(i * blk, blk)
    kpos = lax.broadcasted_iota(jnp.int32, (blk, blk), 0)
    qpos = lax.broadcasted_iota(jnp.int32, (blk, blk), 1)
    causal = kpos <= qpos

    def scores(n, off, rows):
        kb = k_ref[0, pl.ds(off, rows), pair_lanes(heads[n][0])]
        return lax.dot_general(kb, qh[n], nt, preferred_element_type=F32)

    def interleave(stage_scores, stage_probs, stage_values):
        nh = len(heads)
        s_q = {n: stage_scores(n) for n in range(min(AHEAD, nh))}
        p_q = {}
        for t in range(nh + BEHIND):
            if t + AHEAD < nh:
                s_q[t + AHEAD] = stage_scores(t + AHEAD)
            if t < nh:
                p_q[t] = stage_probs(t, s_q.pop(t))
            if t >= BEHIND:
                stage_values(t - BEHIND, *p_q.pop(t - BEHIND))

    def own_probs(n, s):
        s = jnp.where(causal, s, NEG_BIG)
        m0 = jnp.max(s, axis=0, keepdims=True)
        m_ref[n] = m0
        return (jnp.exp2(s - m0).astype(BF16),)

    def own_values(n, pr):
        acc_ref[n] = jnp.dot(vt_ref[n, :, pl.ds(off_i, blk)], pr, preferred_element_type=F32)
        block_gate(n)

    interleave(lambda n: scores(n, off_i, blk), own_probs, own_values)

    def body(jj, _):
        j0 = 2 * jj
        off = pl.multiple_of(j0 * blk, 2 * blk)

        def probs(n, s):
            b0 = bias_ref[n, pl.ds(j0, 1), :]
            b1 = bias_ref[n, pl.ds(j0 + 1, 1), :]
            s0, s1 = s[0:blk], s[blk:]
            mj = jnp.maximum(jnp.max(s0, axis=0, keepdims=True) + b0, jnp.max(s1, axis=0, keepdims=True) + b1)
            m_run = m_ref[n]
            m_new = jnp.maximum(m_run, mj)
            m_ref[n] = m_new
            alpha = jnp.exp2(m_run - m_new)
            p0 = jnp.exp2(s0 - (m_new - b0)).astype(BF16)
            p1 = jnp.exp2(s1 - (m_new - b1)).astype(BF16)
            return alpha, p0, p1

        def accumulate(n, alpha, p0, p1):
            pv = (jnp.dot(vt_ref[n, :, pl.ds(off, blk)], p0, preferred_element_type=F32)
                  + jnp.dot(vt_ref[n, :, pl.ds(off + blk, blk)], p1, preferred_element_type=F32))
            acc_ref[n] = alpha * acc_ref[n] + pv

        interleave(lambda n: scores(n, off, 2 * blk), probs, accumulate)
        return 0

    lax.fori_loop(0, (i + 1) // 2, body, 0)
    for p in range(npair):
        outs = []
        for h in range(2):
            acc = acc_ref[2 * p + h]
            outs.append(acc[0:HEAD_DIM] / acc[HEAD_DIM:HEAD_DIM + 1])
        o_ref[0, :, pair_lanes(p)] = jnp.concatenate(outs, axis=0).T.astype(o_ref.dtype)


def _moba_attention(q, k, v):
    b, s, aw = q.shape
    nb = s // MOBA_BLOCK
    npair = aw // LANES
    nh = 2 * npair
    return pl.pallas_call(
        functools.partial(_attn_kernel, nb=nb, npair=npair),
        grid=(b, nb),
        in_specs=[pl.BlockSpec((1, MOBA_BLOCK, aw), lambda b_, i: (b_, i, 0)),
                  pl.BlockSpec((1, s, aw), lambda b_, i: (b_, 0, 0)),
                  pl.BlockSpec((1, s, aw), lambda b_, i: (b_, 0, 0))],
        out_specs=pl.BlockSpec((1, MOBA_BLOCK, aw), lambda b_, i: (b_, i, 0)),
        out_shape=jax.ShapeDtypeStruct((b, s, aw), BF16),
        scratch_shapes=[pltpu.VMEM((nh, VT_ROWS, s), BF16),
                        pltpu.VMEM((nh, nb, LANES), F32),
                        pltpu.VMEM((nh, nb, MOBA_BLOCK), F32),
                        pltpu.VMEM((nh, 1, MOBA_BLOCK), F32),
                        pltpu.VMEM((nh, VT_ROWS, MOBA_BLOCK), F32)],
        compiler_params=_cparams(("parallel", "arbitrary")),
        name="moba_attn",
    )(q, k, v)


def _rglru_kernel(rec_ref, cw_ref, cb_ref, wg_ref, ba_ref, bi_ref, lam_ref, o_ref,
                  tail_ref, h_ref, a_ref, u_ref, hin_ref, xs_ref, *, rw):
    ts = rec_ref.shape[1]
    t_idx = pl.program_id(1)

    @pl.when(t_idx == 0)
    def _reset():
        tail_ref[...] = jnp.zeros_like(tail_ref)
        h_ref[...] = jnp.zeros_like(h_ref)

    xr = rec_ref[0, :, 0:rw]
    xs_ref[0:SUBLANES, :] = tail_ref[...]
    xs_ref[SUBLANES:, :] = xr
    tail_ref[...] = xr[ts - SUBLANES:, :]
    xc = xr * cw_ref[CONV_WIDTH - 1:CONV_WIDTH, :] + cb_ref[...]
    for d in range(1, CONV_WIDTH):
        xc = xc + xs_ref[SUBLANES - d:SUBLANES - d + ts, :] * cw_ref[CONV_WIDTH - 1 - d:CONV_WIDTH - d, :]

    pre = jnp.dot(xc.astype(BF16), wg_ref[...], preferred_element_type=F32)
    sigmoid = lambda v: 0.5 * jnp.tanh(0.5 * v) + 0.5
    r = sigmoid(pre[:, 0:rw] + ba_ref[...])
    gi = sigmoid(pre[:, rw:] + bi_ref[...])
    lam = lam_ref[...]
    softplus_neg = jnp.maximum(-lam, 0.0) + jnp.log1p(jnp.exp(-jnp.abs(lam)))
    log_a = (-LRU_C * r) * softplus_neg
    a = jnp.exp(log_a)
    z = -jnp.tanh(log_a) * (a * a + 1.0)
    root = jnp.where(z > 0.0, z * lax.rsqrt(z), 0.0)
    u = root * (gi * xc)

    ng = ts // SUBLANES
    sub = lax.broadcasted_iota(jnp.int32, (ts, 1), 0) % SUBLANES
    d = 1
    while d < SUBLANES:
        live = sub >= d
        u = jnp.where(live, a * pltpu.roll(u, d, 0) + u, u)
        a = jnp.where(live, a * pltpu.roll(a, d, 0), a)
        d *= 2
    nplane = rw // LANES
    last_rows = pl.ds(SUBLANES - 1, ng, stride=SUBLANES)
    for c in range(nplane):
        a_ref[c] = a[:, c * LANES:(c + 1) * LANES]
        u_ref[c] = u[:, c * LANES:(c + 1) * LANES]
    ag = jnp.concatenate([a_ref[c, last_rows, :] for c in range(nplane)], axis=1)
    ug = jnp.concatenate([u_ref[c, last_rows, :] for c in range(nplane)], axis=1)
    grow = lax.broadcasted_iota(jnp.int32, (ng, 1), 0)
    d = 1
    while d < ng:
        live = grow >= d
        ug = jnp.where(live, ag * pltpu.roll(ug, d, 0) + ug, ug)
        ag = jnp.where(live, ag * pltpu.roll(ag, d, 0), ag)
        d *= 2
    h0 = h_ref[...]
    hg = ag * h0 + ug
    hin_ref[...] = jnp.where(grow >= 1, pltpu.roll(hg, 1, 0), h0)
    h_ref[...] = hg[ng - 1:, :]
    for g2 in range(0, ng, 2):
        hs = []
        for g in (g2, g2 + 1):
            rows = slice(g * SUBLANES, (g + 1) * SUBLANES)
            a_g = jnp.concatenate([a_ref[c, rows, :] for c in range(nplane)], axis=1)
            u_g = jnp.concatenate([u_ref[c, rows, :] for c in range(nplane)], axis=1)
            hs.append(a_g * hin_ref[g:g + 1, :] + u_g)
        rows = slice(g2 * SUBLANES, (g2 + 2) * SUBLANES)
        gated = jnp.concatenate(hs, axis=0) * jax.nn.gelu(rec_ref[0, rows, rw:])
        o_ref[0, rows, :] = gated.astype(o_ref.dtype)


def _rglru(rec, conv_w, conv_b, w_gates_bf, b_a, b_i, lam):
    b, s, two_rw = rec.shape
    rw = two_rw // 2
    ts = min(ROW_TILE, s)
    full = lambda shape: pl.BlockSpec(shape, lambda b_, t: (0,) * len(shape))
    return pl.pallas_call(
        functools.partial(_rglru_kernel, rw=rw),
        grid=(b, s // ts),
        in_specs=[pl.BlockSpec((1, ts, two_rw), lambda b_, t: (b_, t, 0)),
                  full((CONV_WIDTH, rw)), full((1, rw)), full((rw, two_rw)),
                  full((1, rw)), full((1, rw)), full((1, rw))],
        out_specs=pl.BlockSpec((1, ts, rw), lambda b_, t: (b_, t, 0)),
        out_shape=jax.ShapeDtypeStruct((b, s, rw), BF16),
        scratch_shapes=[pltpu.VMEM((SUBLANES, rw), F32), pltpu.VMEM((1, rw), F32),
                        pltpu.VMEM((rw // LANES, ts, LANES), F32), pltpu.VMEM((rw // LANES, ts, LANES), F32),
                        pltpu.VMEM((ts // SUBLANES, rw), F32), pltpu.VMEM((ts + SUBLANES, rw), F32)],
        compiler_params=_cparams(("parallel", "arbitrary")),
        name="rglru",
    )(rec, conv_w, conv_b, w_gates_bf, b_a, b_i, lam)


def _block_diag(w):
    n, c, d = w.shape
    eye = jnp.eye(n, dtype=w.dtype)
    return (w[:, :, None, :] * eye[:, None, :, None]).reshape(n * c, n * d)


def _layer_norm(z, g, b):
    mu = jnp.mean(z, axis=-1, keepdims=True)
    zc = z - mu
    var = jnp.mean(zc * zc, axis=-1, keepdims=True)
    return zc * lax.rsqrt(var + LN_EPS) * g + b


def _rms(z, g):
    return z * lax.rsqrt(jnp.mean(z * z, axis=-1, keepdims=True) + RMS_EPS) * g


def _route(logits):
    lane = lax.broadcasted_iota(jnp.int32, logits.shape, 1).astype(F32)
    neg_inf = -jnp.inf
    first_lane = lambda mask: jnp.min(jnp.where(mask, lane, float(LANES)), axis=-1, keepdims=True)
    gl = jnp.where(lane < N_GROUPS, logits, neg_inf)
    gmax = jnp.max(gl, axis=-1, keepdims=True)
    gsum = jnp.sum(jnp.exp(gl - gmax), axis=-1, keepdims=True)
    g_p = 1.0 / gsum
    g_idx = first_lane(gl == gmax)
    lo = N_GROUPS + EXPERTS_PER_GROUP * g_idx
    emask = (lane >= lo) & (lane < lo + EXPERTS_PER_GROUP)
    el = jnp.where(emask, logits, neg_inf)
    emax = jnp.max(el, axis=-1, keepdims=True)
    ee = jnp.exp(el - emax)
    ep = ee / jnp.sum(ee, axis=-1, keepdims=True)
    e1 = jnp.max(ep, axis=-1, keepdims=True)
    i1 = first_lane(emask & (ep == e1))
    rest = emask & (lane != i1)
    ep2 = jnp.where(rest, ep, neg_inf)
    e2 = jnp.max(ep2, axis=-1, keepdims=True)
    i2 = first_lane(rest & (ep2 == e2))
    denom = e1 + e2
    w1 = g_p * e1 / denom
    w2 = g_p * e2 / denom
    comb = jnp.where(lane == i1, w1, 0.0) + jnp.where(lane == i2, w2, 0.0)
    return jnp.where(lane == 0, g_idx, comb)


def _outproj_kernel(attn_ref, rec_ref, x_ref, ga_ref, gr_ref, w_ref, g_ref, b_ref, wr_ref, br_ref,
                    x1_ref, gidx_ref, *, alpha, d, n_sub):
    sub = x_ref.shape[0] // n_sub
    rows = [slice(k * sub, (k + 1) * sub) for k in range(n_sub)]
    mixed = [jnp.concatenate([_rms(attn_ref[r, :].astype(F32), ga_ref[...]),
                              _rms(rec_ref[r, :].astype(F32), gr_ref[...])],
                             axis=-1).astype(BF16) for r in rows]
    w = w_ref[...].astype(BF16)
    h = [jnp.dot(m, w, preferred_element_type=F32) for m in mixed]
    logits = []
    for r, hk in zip(rows, h):
        x1 = _layer_norm(alpha * x_ref[r, :] + hk, g_ref[...], b_ref[...])
        x1_ref[r, 0:d] = x1
        logits.append(jnp.dot(x1.astype(BF16), wr_ref[...], preferred_element_type=F32) + br_ref[...])
    pick = ((lax.broadcasted_iota(jnp.int32, (SUBLANES, LANES), 0) == 0)
            & (lax.broadcasted_iota(jnp.int32, (SUBLANES, LANES), 1) == 0)).astype(BF16)
    for r, lg in zip(rows, logits):
        route = _route(lg)
        x1_ref[r, d:] = route
        moved = lax.dot_general(pick, route.astype(BF16), (((1,), (1,)), ((), ())), preferred_element_type=F32)
        gidx_ref[0:1, r] = moved[0:1, :]


def _out_proj(attn2, rec2, x2, g_a, g_r, w_out_all, layer, ln_g, ln_b, w_router_bf, b_router, alpha):
    t, d = x2.shape
    aw, rw = attn2.shape[1], rec2.shape[1]
    tm = min(OUT_TILE, t)
    n_sub = 2 if tm == OUT_TILE else 1
    row = lambda w: pl.BlockSpec((tm, w), lambda i: (i, 0))
    full = lambda shape: pl.BlockSpec(shape, lambda i: (0,) * len(shape))
    return pl.pallas_call(
        functools.partial(_outproj_kernel, alpha=alpha, d=d, n_sub=n_sub),
        grid=(t // tm,),
        in_specs=[row(aw), row(rw), row(d), full((1, aw)), full((1, rw)),
                  pl.BlockSpec((None, aw + rw, d), lambda i: (layer, 0, 0)),
                  full((1, d)), full((1, d)), full((d, LANES)), full((1, LANES))],
        out_specs=[row(d + LANES), pl.BlockSpec((1, tm), lambda i: (0, i))],
        out_shape=[jax.ShapeDtypeStruct((t, d + LANES), F32), jax.ShapeDtypeStruct((1, t), F32)],
        compiler_params=_cparams(("parallel",)),
        name="out_proj_ln_router",
    )(attn2, rec2, x2, g_a, g_r, w_out_all, ln_g, ln_b, w_router_bf, b_router)


def _dispatch_plan(gidx, tm):
    t = gidx.shape[0]
    assert t % tm == 0
    n_tiles = t // tm + N_GROUPS - 1
    counts = jnp.sum((gidx[:, None] == jnp.arange(N_GROUPS)[None, :]).astype(jnp.int32), axis=0)
    tiles_g = (counts + tm - 1) // tm
    tile_end = jnp.cumsum(tiles_g)
    tile_start = tile_end - tiles_g
    n_used = tile_end[-1]
    sorted_start = jnp.cumsum(counts) - counts
    order = jnp.argsort(gidx, stable=True).astype(jnp.int32)
    tile = jnp.arange(n_tiles, dtype=jnp.int32)
    tg = jnp.sum((tile_end[None, :] <= tile[:, None]).astype(jnp.int32), axis=1)
    used = tile < n_used
    tg = jnp.where(used, tg, tg[n_used - 1])
    row_off = (tile - tile_start[tg]) * tm
    nvalid = jnp.where(used, jnp.clip(counts[tg] - row_off, 0, tm), 0).astype(jnp.int32)
    r = jnp.arange(tm, dtype=jnp.int32)[None, :]
    pos = sorted_start[tg][:, None] + row_off[:, None] + jnp.minimum(r, nvalid[:, None] - 1)
    src = order[jnp.clip(pos, 0, t - 1)]
    return src.reshape(-1), tg.astype(jnp.int32), nvalid


def _moe_kernel(src_ref, tg_ref, nv_ref, x_hbm, wg_ref, wu_ref, wd_ref, g_ref, b_ref, out_hbm,
                xbuf0, xbuf1, obuf0, obuf1, gsem, ssem, *, alpha, tm, d, n_tiles):
    i = pl.program_id(0)
    nxt = jnp.minimum(i + 1, n_tiles - 1)
    prv = jnp.maximum(i - 1, 0)
    nv_prv = jnp.where(i > 0, nv_ref[prv], 0)
    pp = jnp.maximum(i - 2, 0)
    nv_pp = jnp.where(i > 1, nv_ref[pp], 0)
    prv_full = nv_prv == tm

    def gather_row(tile, r, xbuf, sem):
        tok = src_ref[tile * tm + r]
        return pltpu.make_async_copy(x_hbm.at[pl.ds(tok, 1), :], xbuf.at[pl.ds(r, 1), :], sem)

    def scatter_row(tile, r, obuf, sem):
        tok = src_ref[tile * tm + r]
        return pltpu.make_async_copy(obuf.at[pl.ds(r, 1), :], out_hbm.at[pl.ds(tok, 1), :], sem)

    gather_tile = lambda xbuf, sem: pltpu.make_async_copy(x_hbm.at[pl.ds(0, tm), :], xbuf.at[pl.ds(0, tm), :], sem)
    scatter_tile = lambda obuf, sem: pltpu.make_async_copy(obuf, out_hbm.at[pl.ds(0, tm), :], sem)

    def for_rows(n, fn, unroll):
        def body(r, _):
            fn(r)
            return 0
        lax.fori_loop(0, n, body, 0, unroll=unroll)

    def step(x_cur, x_nxt, o_cur, o_prv, g_cur, g_nxt, s_cur, s_prv):
        @pl.when(i == 0)
        def _prologue():
            for_rows(tm, lambda r: gather_row(0, r, x_cur, g_cur).start(), 8)
            x_cur[tm:, :] = jnp.zeros((SUBLANES, d + LANES), F32)
            x_nxt[tm:, :] = jnp.zeros((SUBLANES, d + LANES), F32)

        gather_tile(x_cur, g_cur).wait()
        x = x_cur[0:tm, 0:d]
        route = x_cur[0:tm, d:]
        xb = x.astype(BF16)
        lane = lax.broadcasted_iota(jnp.int32, route.shape, 1)
        base = N_GROUPS + EXPERTS_PER_GROUP * tg_ref[i]
        rows_per = tm // EXPERTS_PER_GROUP
        de = wg_ref.shape[2]
        y = jnp.zeros((tm, d), F32)
        for e in range(EXPERTS_PER_GROUP):
            for r in range(e * rows_per, (e + 1) * rows_per):
                gather_row(nxt, r, x_nxt, g_nxt).start()
            zero_row = x_nxt[tm:tm + 1, 0:de]
            gate = jnp.dot(xb, wg_ref[e], preferred_element_type=F32)
            up = jnp.dot(xb, wu_ref[e], preferred_element_type=F32)
            hidden = (jax.nn.silu(gate) * up + zero_row).astype(BF16)
            w_e = jnp.sum(jnp.where(lane == base + e, route, 0.0), axis=-1, keepdims=True)
            y = y + w_e * jnp.dot(hidden, wd_ref[e], preferred_element_type=F32)
        z = alpha * x + y

        @pl.when(nv_pp == tm)
        def _drain_full():
            scatter_tile(o_cur, s_cur).wait()

        @pl.when(nv_pp < tm)
        def _drain_partial():
            for_rows(nv_pp, lambda r: scatter_row(pp, r, o_cur, s_cur).wait(), 1)

        @pl.when(prv_full)
        def _finish_after_full_tile():
            o_cur[...] = _layer_norm(z, g_ref[...], b_ref[...])
            for r in range(tm):
                scatter_row(prv, r, o_prv, s_prv).start()

        @pl.when(jnp.logical_not(prv_full))
        def _finish_after_partial_tile():
            o_cur[...] = _layer_norm(z, g_ref[...], b_ref[...])
            for_rows(nv_prv, lambda r: scatter_row(prv, r, o_prv, s_prv).start(), 1)

        @pl.when(i == n_tiles - 1)
        def _epilogue():
            gather_tile(x_nxt, g_nxt).wait()
            for_rows(nv_prv, lambda r: scatter_row(prv, r, o_prv, s_prv).wait(), 1)
            for_rows(nv_ref[i], lambda r: scatter_row(i, r, o_cur, s_cur).start(), 1)
            for_rows(nv_ref[i], lambda r: scatter_row(i, r, o_cur, s_cur).wait(), 1)

    @pl.when(i % 2 == 0)
    def _even():
        step(xbuf0, xbuf1, obuf0, obuf1, gsem.at[0], gsem.at[1], ssem.at[0], ssem.at[1])

    @pl.when(i % 2 == 1)
    def _odd():
        step(xbuf1, xbuf0, obuf1, obuf0, gsem.at[1], gsem.at[0], ssem.at[1], ssem.at[0])


def _moe(x1e, gidx, t, wg_bf, wu_bf, wd_bf, layer, ln_g, ln_b, alpha, d):
    de = wg_bf.shape[3]
    tm = min(ROW_TILE, t)
    src, tg, nvalid = _dispatch_plan(gidx, tm)
    n_tiles = tg.shape[0]
    epg = EXPERTS_PER_GROUP
    full = lambda shape: pl.BlockSpec(shape, lambda i, *_: (0,) * len(shape))
    grid_spec = pltpu.PrefetchScalarGridSpec(
        num_scalar_prefetch=3,
        grid=(n_tiles,),
        in_specs=[pl.BlockSpec(memory_space=pl.ANY),
                  pl.BlockSpec((None, epg, d, de), lambda i, src_, tg_, nv_: (layer, tg_[i], 0, 0)),
                  pl.BlockSpec((None, epg, d, de), lambda i, src_, tg_, nv_: (layer, tg_[i], 0, 0)),
                  pl.BlockSpec((None, epg, de, d), lambda i, src_, tg_, nv_: (layer, tg_[i], 0, 0)),
                  full((1, d)), full((1, d))],
        out_specs=pl.BlockSpec(memory_space=pl.ANY),
        scratch_shapes=[pltpu.VMEM((tm + SUBLANES, d + LANES), F32), pltpu.VMEM((tm + SUBLANES, d + LANES), F32),
                        pltpu.VMEM((tm, d), F32), pltpu.VMEM((tm, d), F32),
                        pltpu.SemaphoreType.DMA((2,)),
                        pltpu.SemaphoreType.DMA((2,))],
    )
    return pl.pallas_call(
        functools.partial(_moe_kernel, alpha=alpha, tm=tm, d=d, n_tiles=n_tiles),
        grid_spec=grid_spec,
        out_shape=jax.ShapeDtypeStruct((t, d), F32),
        compiler_params=_cparams(("arbitrary",)),
        name="moe_ln",
    )(src, tg, nvalid, x1e, wg_bf, wu_bf, wd_bf, ln_g, ln_b)


def kernel(x, w_in, conv_w, conv_b, w_rg_a, b_rg_a, w_rg_i, b_rg_i, lru_lambda, g_attn_norm, g_rec_norm, w_out,
           ln1_g, ln1_b, w_router_group, b_router_group, w_router_expert, b_router_expert, w_gate, w_up, w_down,
           ln2_g, ln2_b):
    b, s, d = x.shape
    depth = w_in.shape[0]
    rw = conv_w.shape[2]
    aw = (w_in.shape[2] - 2 * rw) // 3
    alpha = (2 * depth) ** 0.25
    tables = _rope_tables(s)
    row = lambda v: v.reshape(1, -1)

    wg_bf, wu_bf, wd_bf = w_gate.astype(BF16), w_up.astype(BF16), w_down.astype(BF16)
    x2 = x.reshape(b * s, d)
    for l in range(depth):
        q, k, v, rec = _in_proj(x2, w_in, l, tables, s, aw)
        attn = _moba_attention(q.reshape(b, s, aw), k.reshape(b, s, aw), v.reshape(b, s, aw))
        w_gates = jnp.concatenate([_block_diag(w_rg_a[l]), _block_diag(w_rg_i[l])], axis=1).astype(BF16)
        rec_out = _rglru(rec.reshape(b, s, 2 * rw), conv_w[l], row(conv_b[l]), w_gates,
                         row(b_rg_a[l]), row(b_rg_i[l]), row(lru_lambda[l]))
        n_route = N_GROUPS + N_GROUPS * EXPERTS_PER_GROUP
        w_router = jnp.concatenate([w_router_group[l], w_router_expert[l]], axis=1)
        w_router = jnp.pad(w_router, ((0, 0), (0, LANES - n_route))).astype(BF16)
        b_router = jnp.pad(jnp.concatenate([b_router_group[l], b_router_expert[l]]), (0, LANES - n_route))
        x1e, gidx = _out_proj(attn.reshape(b * s, aw), rec_out.reshape(b * s, rw), x2,
                        row(g_attn_norm[l]), row(g_rec_norm[l]), w_out, l,
                        row(ln1_g[l]), row(ln1_b[l]), w_router, row(b_router), alpha)
        x2 = _moe(x1e, gidx[0].astype(jnp.int32), b * s, wg_bf, wu_bf, wd_bf, l,
                  row(ln2_g[l]), row(ln2_b[l]), alpha, d)
    return x2.reshape(b, s, d)
```

```python
import functools

import jax
import jax.numpy as jnp
from jax import lax
from jax.experimental import pallas as pl
from jax.experimental.pallas import tpu as pltpu

F32 = jnp.float32
BF16 = jnp.bfloat16

HEAD_DIM = 64
ROT_DIM = HEAD_DIM // 4
ROPE_THETA = 500000.0
MOBA_BLOCK = 256
MOBA_TOPK = 3
CONV_WIDTH = 4
LRU_C = 8.0
N_GROUPS = 4
EXPERTS_PER_GROUP = 4
LN_EPS = 1e-5
RMS_EPS = 1e-6

LANES = 128
SUBLANES = 8
NEG_BIG = -1e30
LOG2E = 1.4426950408889634
Q_SCALE = HEAD_DIM ** -0.5 * LOG2E
AHEAD, BEHIND = 3, 1
VT_ROWS = HEAD_DIM + 16
VMEM_LIMIT = 56 * 1024 * 1024
ROW_TILE = 512
OUT_TILE = 1024


def _cparams(sem):
    return pltpu.CompilerParams(dimension_semantics=sem, vmem_limit_bytes=VMEM_LIMIT)


def _inproj_kernel(x_ref, w_ref, c_ref, s1_ref, s2_ref, q_ref, k_ref, v_ref, rec_ref, *, aw):
    xb = x_ref[...].astype(BF16)
    cos, sin_lo, sin_hi = c_ref[...], s1_ref[...], s2_ref[...]

    def rope(t):
        outs = []
        for c in range(aw // LANES):
            tc = t[:, c * LANES:(c + 1) * LANES]
            outs.append(tc * cos + pltpu.roll(tc, 8, 1) * sin_lo + pltpu.roll(tc, LANES - 8, 1) * sin_hi)
        return jnp.concatenate(outs, axis=1)

    q = jnp.dot(xb, w_ref[:, 0:aw].astype(BF16), preferred_element_type=F32)
    q_ref[...] = (rope(q) * Q_SCALE).astype(BF16)
    k = jnp.dot(xb, w_ref[:, aw:2 * aw].astype(BF16), preferred_element_type=F32)
    k_ref[...] = rope(k).astype(BF16)
    v = jnp.dot(xb, w_ref[:, 2 * aw:3 * aw].astype(BF16), preferred_element_type=F32)
    v_ref[...] = v.astype(BF16)
    rec_ref[...] = jnp.dot(xb, w_ref[:, 3 * aw:].astype(BF16), preferred_element_type=F32)


def _rope_tables(seq):
    half = ROT_DIM // 2
    inv_freq = ROPE_THETA ** (-jnp.arange(half, dtype=F32) * 2.0 / ROT_DIM)
    ang = jnp.arange(seq).astype(F32)[:, None] * inv_freq[None, :]
    cos, sin = jnp.cos(ang), jnp.sin(ang)
    zeros = jnp.zeros((seq, HEAD_DIM - ROT_DIM), F32)
    zh = jnp.zeros((seq, half), F32)
    c64 = jnp.concatenate([cos, cos, zeros + 1.0], axis=1)
    lo64 = jnp.concatenate([zh, sin, zeros], axis=1)
    hi64 = jnp.concatenate([-sin, zh, zeros], axis=1)
    rep = LANES // HEAD_DIM
    return jnp.tile(c64, (1, rep)), jnp.tile(lo64, (1, rep)), jnp.tile(hi64, (1, rep))


def _in_proj(x2, w_all, layer, tables, seq, aw):
    t, d = x2.shape
    n = w_all.shape[2]
    tm = min(ROW_TILE, seq)
    nseq = seq // tm
    tab_spec = pl.BlockSpec((tm, LANES), lambda i: (i % nseq, 0))
    return pl.pallas_call(
        functools.partial(_inproj_kernel, aw=aw),
        grid=(t // tm,),
        in_specs=[pl.BlockSpec((tm, d), lambda i: (i, 0)),
                  pl.BlockSpec((None, d, n), lambda i: (layer, 0, 0)),
                  tab_spec, tab_spec, tab_spec],
        out_specs=[pl.BlockSpec((tm, aw), lambda i: (i, 0)),
                   pl.BlockSpec((tm, aw), lambda i: (i, 0)),
                   pl.BlockSpec((tm, aw), lambda i: (i, 0)),
                   pl.BlockSpec((tm, n - 3 * aw), lambda i: (i, 0))],
        out_shape=[jax.ShapeDtypeStruct((t, aw), BF16),
                   jax.ShapeDtypeStruct((t, aw), BF16),
                   jax.ShapeDtypeStruct((t, aw), BF16),
                   jax.ShapeDtypeStruct((t, n - 3 * aw), F32)],
        compiler_params=_cparams(("parallel",)),
        name="in_proj",
    )(x2, w_all, *tables)


def _attn_kernel(q_ref, k_ref, v_ref, o_ref, vt_ref, km_ref, bias_ref, m_ref, acc_ref, *, nb, npair):
    blk = MOBA_BLOCK
    i = pl.program_id(1)
    lane = lax.broadcasted_iota(jnp.int32, (1, LANES), 1)
    head_lanes = (lane < HEAD_DIM, lane >= HEAD_DIM)
    nt = (((1,), (1,)), ((), ()))
    heads = [(p, h) for p in range(npair) for h in range(2)]
    pair_lanes = lambda p: slice(p * LANES, (p + 1) * LANES)

    @pl.when(i == 0)
    def _prepare():
        def prep(j, _):
            off = pl.multiple_of(j * blk, blk)
            for p in range(npair):
                kb = k_ref[0, pl.ds(off, blk), pair_lanes(p)].astype(F32)
                km = jnp.sum(kb, axis=0, keepdims=True) * (1.0 / blk)
                km_ref[2 * p, pl.ds(j, 1), :] = jnp.where(head_lanes[0], km, 0.0)
                km_ref[2 * p + 1, pl.ds(j, 1), :] = jnp.where(head_lanes[1], km, 0.0)
                vt = v_ref[0, pl.ds(off, blk), pair_lanes(p)].astype(F32).T
                vt_ref[2 * p, 0:HEAD_DIM, pl.ds(off, blk)] = vt[0:HEAD_DIM].astype(BF16)
                vt_ref[2 * p + 1, 0:HEAD_DIM, pl.ds(off, blk)] = vt[HEAD_DIM:].astype(BF16)
            return 0

        lax.fori_loop(0, nb, prep, 0)
        vt_ref[:, HEAD_DIM:, :] = jnp.ones((2 * npair, VT_ROWS - HEAD_DIM, nb * blk), BF16)

    qh = []
    for p, h in heads:
        q_pair = q_ref[0, :, pair_lanes(p)]
        qh.append(jnp.where(head_lanes[h], q_pair, jnp.zeros_like(q_pair)))

    def block_gate(n):
        p, h = heads[n]
        n_iota = lax.broadcasted_iota(jnp.int32, (nb, blk), 0)
        valid = n_iota < i
        g = lax.dot_general(km_ref[n], q_ref[0, :, pair_lanes(p)].astype(F32), nt, preferred_element_type=F32)
        g = jnp.where(valid, g, -jnp.inf)
        bias = jnp.full((nb, blk), NEG_BIG, F32)
        for _ in range(min(MOBA_TOPK, nb)):
            top = jnp.max(g, axis=0, keepdims=True)
            first = jnp.min(jnp.where(g == top, n_iota, nb), axis=0, keepdims=True)
            hit = n_iota == first
            bias = jnp.where(hit & valid, 0.0, bias)
            g = jnp.where(hit, -jnp.inf, g)
        bias_ref[n] = bias

    off_i = pl.multiple_of(i * blk, blk)
    kpos = lax.broadcasted_iota(jnp.int32, (blk, blk), 0)
    qpos = lax.broadcasted_iota(jnp.int32, (blk, blk), 1)
    causal = kpos <= qpos

    def scores(n, off, rows):
        kb = k_ref[0, pl.ds(off, rows), pair_lanes(heads[n][0])]
        return lax.dot_general(kb, qh[n], nt, preferred_element_type=F32)

    def interleave(stage_scores, stage_probs, stage_values, count):
        s_q = {n: stage_scores(n) for n in range(min(AHEAD, count))}
        p_q = {}
        for t in range(count + BEHIND):
            if t + AHEAD < count:
                s_q[t + AHEAD] = stage_scores(t + AHEAD)
            if t < count:
                p_q[t] = stage_probs(t, s_q.pop(t))
            if t >= BEHIND:
                stage_values(t - BEHIND, *p_q.pop(t - BEHIND))

    def own_probs(n, s):
        s = jnp.where(causal, s, NEG_BIG)
        m0 = jnp.max(s, axis=0, keepdims=True)
        m_ref[n] = m0
        return (jnp.exp2(s - m0).astype(BF16),)

    def own_values(n, pr):
        acc_ref[n] = jnp.dot(vt_ref[n, :, pl.ds(off_i, blk)], pr, preferred_element_type=F32)
        block_gate(n)

    nh = len(heads)
    interleave(lambda n: scores(n, off_i, blk), own_probs, own_values, nh)

    def run_pairs(first_pair, n_pairs):
        def j0_of(t):
            return 2 * (first_pair + t // nh)

        def off_of(t):
            return pl.multiple_of(j0_of(t) * blk, 2 * blk)

        def probs(t, s):
            n, j0 = t % nh, j0_of(t)
            b0 = bias_ref[n, pl.ds(j0, 1), :]
            b1 = bias_ref[n, pl.ds(j0 + 1, 1), :]
            s0, s1 = s[0:blk], s[blk:]
            mj = jnp.maximum(jnp.max(s0, axis=0, keepdims=True) + b0, jnp.max(s1, axis=0, keepdims=True) + b1)
            m_run = m_ref[n]
            m_new = jnp.maximum(m_run, mj)
            m_ref[n] = m_new
            alpha = jnp.exp2(m_run - m_new)
            p0 = jnp.exp2(s0 - (m_new - b0)).astype(BF16)
            p1 = jnp.exp2(s1 - (m_new - b1)).astype(BF16)
            return alpha, p0, p1

        def accumulate(t, alpha, p0, p1):
            n, off = t % nh, off_of(t)
            pv = (jnp.dot(vt_ref[n, :, pl.ds(off, blk)], p0, preferred_element_type=F32)
                  + jnp.dot(vt_ref[n, :, pl.ds(off + blk, blk)], p1, preferred_element_type=F32))
            acc_ref[n] = alpha * acc_ref[n] + pv

        interleave(lambda t: scores(t % nh, off_of(t), 2 * blk), probs, accumulate, n_pairs * nh)

    n_pairs = (i + 1) // 2

    def quad_body(jj, _):
        run_pairs(4 * jj, 4)
        return 0

    lax.fori_loop(0, n_pairs // 4, quad_body, 0)

    @pl.when((n_pairs // 2) % 2 == 1)
    def _pair_of_pairs():
        run_pairs(4 * (n_pairs // 4), 2)

    @pl.when(n_pairs % 2 == 1)
    def _last_pair():
        run_pairs(n_pairs - 1, 1)

    for p in range(npair):
        outs = []
        for h in range(2):
            acc = acc_ref[2 * p + h]
            outs.append(acc[0:HEAD_DIM] / acc[HEAD_DIM:HEAD_DIM + 1])
        o_ref[0, :, pair_lanes(p)] = jnp.concatenate(outs, axis=0).T.astype(o_ref.dtype)


def _moba_attention(q, k, v):
    b, s, aw = q.shape
    nb = s // MOBA_BLOCK
    npair = aw // LANES
    nh = 2 * npair
    return pl.pallas_call(
        functools.partial(_attn_kernel, nb=nb, npair=npair),
        grid=(b, nb),
        in_specs=[pl.BlockSpec((1, MOBA_BLOCK, aw), lambda b_, i: (b_, i, 0)),
                  pl.BlockSpec((1, s, aw), lambda b_, i: (b_, 0, 0)),
                  pl.BlockSpec((1, s, aw), lambda b_, i: (b_, 0, 0))],
        out_specs=pl.BlockSpec((1, MOBA_BLOCK, aw), lambda b_, i: (b_, i, 0)),
        out_shape=jax.ShapeDtypeStruct((b, s, aw), BF16),
        scratch_shapes=[pltpu.VMEM((nh, VT_ROWS, s), BF16),
                        pltpu.VMEM((nh, nb, LANES), F32),
                        pltpu.VMEM((nh, nb, MOBA_BLOCK), F32),
                        pltpu.VMEM((nh, 1, MOBA_BLOCK), F32),
                        pltpu.VMEM((nh, VT_ROWS, MOBA_BLOCK), F32)],
        compiler_params=_cparams(("parallel", "arbitrary")),
        name="moba_attn",
    )(q, k, v)


def _rglru_kernel(rec_ref, cw_ref, cb_ref, wg_ref, ba_ref, bi_ref, lam_ref, o_ref,
                  tail_ref, h_ref, a_ref, u_ref, hin_ref, xs_ref, *, rw):
    ts = rec_ref.shape[1]
    t_idx = pl.program_id(1)

    @pl.when(t_idx == 0)
    def _reset():
        tail_ref[...] = jnp.zeros_like(tail_ref)
        h_ref[...] = jnp.zeros_like(h_ref)

    xr = rec_ref[0, :, 0:rw]
    xs_ref[0:SUBLANES, :] = tail_ref[...]
    xs_ref[SUBLANES:, :] = xr
    tail_ref[...] = xr[ts - SUBLANES:, :]
    xc = xr * cw_ref[CONV_WIDTH - 1:CONV_WIDTH, :] + cb_ref[...]
    for d in range(1, CONV_WIDTH):
        xc = xc + xs_ref[SUBLANES - d:SUBLANES - d + ts, :] * cw_ref[CONV_WIDTH - 1 - d:CONV_WIDTH - d, :]

    pre = jnp.dot(xc.astype(BF16), wg_ref[...], preferred_element_type=F32)
    sigmoid = lambda v: 0.5 * jnp.tanh(0.5 * v) + 0.5
    r = sigmoid(pre[:, 0:rw] + ba_ref[...])
    gi = sigmoid(pre[:, rw:] + bi_ref[...])
    lam = lam_ref[...]
    softplus_neg = jnp.maximum(-lam, 0.0) + jnp.log1p(jnp.exp(-jnp.abs(lam)))
    log_a = (-LRU_C * r) * softplus_neg
    a = jnp.exp(log_a)
    z = -jnp.tanh(log_a) * (a * a + 1.0)
    root = jnp.where(z > 0.0, z * lax.rsqrt(z), 0.0)
    u = root * (gi * xc)

    ng = ts // SUBLANES
    sub = lax.broadcasted_iota(jnp.int32, (ts, 1), 0) % SUBLANES
    d = 1
    while d < SUBLANES:
        live = sub >= d
        u = jnp.where(live, a * pltpu.roll(u, d, 0) + u, u)
        a = jnp.where(live, a * pltpu.roll(a, d, 0), a)
        d *= 2
    nplane = rw // LANES
    last_rows = pl.ds(SUBLANES - 1, ng, stride=SUBLANES)
    for c in range(nplane):
        a_ref[c] = a[:, c * LANES:(c + 1) * LANES]
        u_ref[c] = u[:, c * LANES:(c + 1) * LANES]
    ag = jnp.concatenate([a_ref[c, last_rows, :] for c in range(nplane)], axis=1)
    ug = jnp.concatenate([u_ref[c, last_rows, :] for c in range(nplane)], axis=1)
    grow = lax.broadcasted_iota(jnp.int32, (ng, 1), 0)
    d = 1
    while d < ng:
        live = grow >= d
        ug = jnp.where(live, ag * pltpu.roll(ug, d, 0) + ug, ug)
        ag = jnp.where(live, ag * pltpu.roll(ag, d, 0), ag)
        d *= 2
    h0 = h_ref[...]
    hg = ag * h0 + ug
    hin_ref[...] = jnp.where(grow >= 1, pltpu.roll(hg, 1, 0), h0)
    h_ref[...] = hg[ng - 1:, :]
    for g2 in range(0, ng, 2):
        hs = []
        for g in (g2, g2 + 1):
            rows = slice(g * SUBLANES, (g + 1) * SUBLANES)
            a_g = jnp.concatenate([a_ref[c, rows, :] for c in range(nplane)], axis=1)
            u_g = jnp.concatenate([u_ref[c, rows, :] for c in range(nplane)], axis=1)
            hs.append(a_g * hin_ref[g:g + 1, :] + u_g)
        rows = slice(g2 * SUBLANES, (g2 + 2) * SUBLANES)
        gated = jnp.concatenate(hs, axis=0) * jax.nn.gelu(rec_ref[0, rows, rw:])
        o_ref[0, rows, :] = gated.astype(o_ref.dtype)


def _rglru(rec, conv_w, conv_b, w_gates_bf, b_a, b_i, lam):
    b, s, two_rw = rec.shape
    rw = two_rw // 2
    ts = min(ROW_TILE, s)
    full = lambda shape: pl.BlockSpec(shape, lambda b_, t: (0,) * len(shape))
    return pl.pallas_call(
        functools.partial(_rglru_kernel, rw=rw),
        grid=(b, s // ts),
        in_specs=[pl.BlockSpec((1, ts, two_rw), lambda b_, t: (b_, t, 0)),
                  full((CONV_WIDTH, rw)), full((1, rw)), full((rw, two_rw)),
                  full((1, rw)), full((1, rw)), full((1, rw))],
        out_specs=pl.BlockSpec((1, ts, rw), lambda b_, t: (b_, t, 0)),
        out_shape=jax.ShapeDtypeStruct((b, s, rw), BF16),
        scratch_shapes=[pltpu.VMEM((SUBLANES, rw), F32), pltpu.VMEM((1, rw), F32),
                        pltpu.VMEM((rw // LANES, ts, LANES), F32), pltpu.VMEM((rw // LANES, ts, LANES), F32),
                        pltpu.VMEM((ts // SUBLANES, rw), F32), pltpu.VMEM((ts + SUBLANES, rw), F32)],
        compiler_params=_cparams(("parallel", "arbitrary")),
        name="rglru",
    )(rec, conv_w, conv_b, w_gates_bf, b_a, b_i, lam)


def _block_diag(w):
    n, c, d = w.shape
    eye = jnp.eye(n, dtype=w.dtype)
    return (w[:, :, None, :] * eye[:, None, :, None]).reshape(n * c, n * d)


def _layer_norm(z, g, b):
    mu = jnp.mean(z, axis=-1, keepdims=True)
    zc = z - mu
    var = jnp.mean(zc * zc, axis=-1, keepdims=True)
    return zc * lax.rsqrt(var + LN_EPS) * g + b


def _rms(z, g):
    return z * lax.rsqrt(jnp.mean(z * z, axis=-1, keepdims=True) + RMS_EPS) * g


def _route(logits):
    lane = lax.broadcasted_iota(jnp.int32, logits.shape, 1).astype(F32)
    neg_inf = -jnp.inf
    first_lane = lambda mask: jnp.min(jnp.where(mask, lane, float(LANES)), axis=-1, keepdims=True)
    gl = jnp.where(lane < N_GROUPS, logits, neg_inf)
    gmax = jnp.max(gl, axis=-1, keepdims=True)
    gsum = jnp.sum(jnp.exp(gl - gmax), axis=-1, keepdims=True)
    g_p = 1.0 / gsum
    g_idx = first_lane(gl == gmax)
    lo = N_GROUPS + EXPERTS_PER_GROUP * g_idx
    emask = (lane >= lo) & (lane < lo + EXPERTS_PER_GROUP)
    el = jnp.where(emask, logits, neg_inf)
    emax = jnp.max(el, axis=-1, keepdims=True)
    ee = jnp.exp(el - emax)
    ep = ee / jnp.sum(ee, axis=-1, keepdims=True)
    e1 = jnp.max(ep, axis=-1, keepdims=True)
    i1 = first_lane(emask & (ep == e1))
    rest = emask & (lane != i1)
    ep2 = jnp.where(rest, ep, neg_inf)
    e2 = jnp.max(ep2, axis=-1, keepdims=True)
    i2 = first_lane(rest & (ep2 == e2))
    denom = e1 + e2
    w1 = g_p * e1 / denom
    w2 = g_p * e2 / denom
    comb = jnp.where(lane == i1, w1, 0.0) + jnp.where(lane == i2, w2, 0.0)
    return jnp.where(lane == 0, g_idx, comb)


def _outproj_kernel(attn_ref, rec_ref, x_ref, ga_ref, gr_ref, w_ref, g_ref, b_ref, wr_ref, br_ref,
                    x1_ref, gidx_ref, *, alpha, d, n_sub):
    sub = x_ref.shape[0] // n_sub
    rows = [slice(k * sub, (k + 1) * sub) for k in range(n_sub)]
    mixed = [jnp.concatenate([_rms(attn_ref[r, :].astype(F32), ga_ref[...]),
                              _rms(rec_ref[r, :].astype(F32), gr_ref[...])],
                             axis=-1).astype(BF16) for r in rows]
    w = w_ref[...].astype(BF16)
    h = [jnp.dot(m, w, preferred_element_type=F32) for m in mixed]
    logits = []
    for r, hk in zip(rows, h):
        x1 = _layer_norm(alpha * x_ref[r, :] + hk, g_ref[...], b_ref[...])
        x1_ref[r, 0:d] = x1
        logits.append(jnp.dot(x1.astype(BF16), wr_ref[...], preferred_element_type=F32) + br_ref[...])
    pick = ((lax.broadcasted_iota(jnp.int32, (SUBLANES, LANES), 0) == 0)
            & (lax.broadcasted_iota(jnp.int32, (SUBLANES, LANES), 1) == 0)).astype(BF16)
    for r, lg in zip(rows, logits):
        route = _route(lg)
        x1_ref[r, d:] = route
        moved = lax.dot_general(pick, route.astype(BF16), (((1,), (1,)), ((), ())), preferred_element_type=F32)
        gidx_ref[0:1, r] = moved[0:1, :]


def _out_proj(attn2, rec2, x2, g_a, g_r, w_out_all, layer, ln_g, ln_b, w_router_bf, b_router, alpha):
    t, d = x2.shape
    aw, rw = attn2.shape[1], rec2.shape[1]
    tm = min(OUT_TILE, t)
    n_sub = 2 if tm == OUT_TILE else 1
    row = lambda w: pl.BlockSpec((tm, w), lambda i: (i, 0))
    full = lambda shape: pl.BlockSpec(shape, lambda i: (0,) * len(shape))
    return pl.pallas_call(
        functools.partial(_outproj_kernel, alpha=alpha, d=d, n_sub=n_sub),
        grid=(t // tm,),
        in_specs=[row(aw), row(rw), row(d), full((1, aw)), full((1, rw)),
                  pl.BlockSpec((None, aw + rw, d), lambda i: (layer, 0, 0)),
                  full((1, d)), full((1, d)), full((d, LANES)), full((1, LANES))],
        out_specs=[row(d + LANES), pl.BlockSpec((1, tm), lambda i: (0, i))],
        out_shape=[jax.ShapeDtypeStruct((t, d + LANES), F32), jax.ShapeDtypeStruct((1, t), F32)],
        compiler_params=_cparams(("parallel",)),
        name="out_proj_ln_router",
    )(attn2, rec2, x2, g_a, g_r, w_out_all, ln_g, ln_b, w_router_bf, b_router)


def _dispatch_plan(gidx, tm):
    t = gidx.shape[0]
    assert t % tm == 0
    n_tiles = t // tm + N_GROUPS - 1
    counts = jnp.sum((gidx[:, None] == jnp.arange(N_GROUPS)[None, :]).astype(jnp.int32), axis=0)
    tiles_g = (counts + tm - 1) // tm
    tile_end = jnp.cumsum(tiles_g)
    tile_start = tile_end - tiles_g
    n_used = tile_end[-1]
    sorted_start = jnp.cumsum(counts) - counts
    order = jnp.argsort(gidx, stable=True).astype(jnp.int32)
    tile = jnp.arange(n_tiles, dtype=jnp.int32)
    tg = jnp.sum((tile_end[None, :] <= tile[:, None]).astype(jnp.int32), axis=1)
    used = tile < n_used
    tg = jnp.where(used, tg, tg[n_used - 1])
    row_off = (tile - tile_start[tg]) * tm
    nvalid = jnp.where(used, jnp.clip(counts[tg] - row_off, 0, tm), 0).astype(jnp.int32)
    r = jnp.arange(tm, dtype=jnp.int32)[None, :]
    pos = sorted_start[tg][:, None] + row_off[:, None] + jnp.minimum(r, nvalid[:, None] - 1)
    src = order[jnp.clip(pos, 0, t - 1)]
    return src.reshape(-1), tg.astype(jnp.int32), nvalid


def _moe_kernel(src_ref, tg_ref, nv_ref, x_hbm, wg_ref, wu_ref, wd_ref, g_ref, b_ref, out_hbm,
                xbuf0, xbuf1, obuf0, obuf1, gsem, ssem, *, alpha, tm, d, n_tiles):
    i = pl.program_id(0)
    nxt = jnp.minimum(i + 1, n_tiles - 1)
    prv = jnp.maximum(i - 1, 0)
    nv_prv = jnp.where(i > 0, nv_ref[prv], 0)
    pp = jnp.maximum(i - 2, 0)
    nv_pp = jnp.where(i > 1, nv_ref[pp], 0)
    prv_full = nv_prv == tm

    def gather_row(tile, r, xbuf, sem):
        tok = src_ref[tile * tm + r]
        return pltpu.make_async_copy(x_hbm.at[pl.ds(tok, 1), :], xbuf.at[pl.ds(r, 1), :], sem)

    def scatter_row(tile, r, obuf, sem):
        tok = src_ref[tile * tm + r]
        return pltpu.make_async_copy(obuf.at[pl.ds(r, 1), :], out_hbm.at[pl.ds(tok, 1), :], sem)

    gather_tile = lambda xbuf, sem: pltpu.make_async_copy(x_hbm.at[pl.ds(0, tm), :], xbuf.at[pl.ds(0, tm), :], sem)
    scatter_tile = lambda obuf, sem: pltpu.make_async_copy(obuf, out_hbm.at[pl.ds(0, tm), :], sem)

    def for_rows(n, fn, unroll):
        def body(r, _):
            fn(r)
            return 0
        lax.fori_loop(0, n, body, 0, unroll=unroll)

    def step(x_cur, x_nxt, o_cur, o_prv, g_cur, g_nxt, s_cur, s_prv):
        @pl.when(i == 0)
        def _prologue():
            for_rows(tm, lambda r: gather_row(0, r, x_cur, g_cur).start(), 8)
            x_cur[tm:, :] = jnp.zeros((SUBLANES, d + LANES), F32)
            x_nxt[tm:, :] = jnp.zeros((SUBLANES, d + LANES), F32)

        gather_tile(x_cur, g_cur).wait()
        x = x_cur[0:tm, 0:d]
        route = x_cur[0:tm, d:]
        xb = x.astype(BF16)
        lane = lax.broadcasted_iota(jnp.int32, route.shape, 1)
        base = N_GROUPS + EXPERTS_PER_GROUP * tg_ref[i]
        rows_per = tm // EXPERTS_PER_GROUP
        de = wg_ref.shape[2]
        y = jnp.zeros((tm, d), F32)
        for e in range(EXPERTS_PER_GROUP):
            for r in range(e * rows_per, (e + 1) * rows_per):
                gather_row(nxt, r, x_nxt, g_nxt).start()
            zero_row = x_nxt[tm:tm + 1, 0:de]
            gate = jnp.dot(xb, wg_ref[e], preferred_element_type=F32)
            up = jnp.dot(xb, wu_ref[e], preferred_element_type=F32)
            hidden = (jax.nn.silu(gate) * up + zero_row).astype(BF16)
            w_e = jnp.sum(jnp.where(lane == base + e, route, 0.0), axis=-1, keepdims=True)
            y = y + w_e * jnp.dot(hidden, wd_ref[e], preferred_element_type=F32)
        z = alpha * x + y

        @pl.when(nv_pp == tm)
        def _drain_full():
            scatter_tile(o_cur, s_cur).wait()

        @pl.when(nv_pp < tm)
        def _drain_partial():
            for_rows(nv_pp, lambda r: scatter_row(pp, r, o_cur, s_cur).wait(), 1)

        @pl.when(prv_full)
        def _finish_after_full_tile():
            o_cur[...] = _layer_norm(z, g_ref[...], b_ref[...])
            for r in range(tm):
                scatter_row(prv, r, o_prv, s_prv).start()

        @pl.when(jnp.logical_not(prv_full))
        def _finish_after_partial_tile():
            o_cur[...] = _layer_norm(z, g_ref[...], b_ref[...])
            for_rows(nv_prv, lambda r: scatter_row(prv, r, o_prv, s_prv).start(), 1)

        @pl.when(i == n_tiles - 1)
        def _epilogue():
            gather_tile(x_nxt, g_nxt).wait()
            for_rows(nv_prv, lambda r: scatter_row(prv, r, o_prv, s_prv).wait(), 1)
            for_rows(nv_ref[i], lambda r: scatter_row(i, r, o_cur, s_cur).start(), 1)
            for_rows(nv_ref[i], lambda r: scatter_row(i, r, o_cur, s_cur).wait(), 1)

    @pl.when(i % 2 == 0)
    def _even():
        step(xbuf0, xbuf1, obuf0, obuf1, gsem.at[0], gsem.at[1], ssem.at[0], ssem.at[1])

    @pl.when(i % 2 == 1)
    def _odd():
        step(xbuf1, xbuf0, obuf1, obuf0, gsem.at[1], gsem.at[0], ssem.at[1], ssem.at[0])


def _moe(x1e, gidx, t, wg_bf, wu_bf, wd_bf, layer, ln_g, ln_b, alpha, d):
    de = wg_bf.shape[3]
    tm = min(ROW_TILE, t)
    src, tg, nvalid = _dispatch_plan(gidx, tm)
    n_tiles = tg.shape[0]
    epg = EXPERTS_PER_GROUP
    full = lambda shape: pl.BlockSpec(shape, lambda i, *_: (0,) * len(shape))
    grid_spec = pltpu.PrefetchScalarGridSpec(
        num_scalar_prefetch=3,
        grid=(n_tiles,),
        in_specs=[pl.BlockSpec(memory_space=pl.ANY),
                  pl.BlockSpec((None, epg, d, de), lambda i, src_, tg_, nv_: (layer, tg_[i], 0, 0)),
                  pl.BlockSpec((None, epg, d, de), lambda i, src_, tg_, nv_: (layer, tg_[i], 0, 0)),
                  pl.BlockSpec((None, epg, de, d), lambda i, src_, tg_, nv_: (layer, tg_[i], 0, 0)),
                  full((1, d)), full((1, d))],
        out_specs=pl.BlockSpec(memory_space=pl.ANY),
        scratch_shapes=[pltpu.VMEM((tm + SUBLANES, d + LANES), F32), pltpu.VMEM((tm + SUBLANES, d + LANES), F32),
                        pltpu.VMEM((tm, d), F32), pltpu.VMEM((tm, d), F32),
                        pltpu.SemaphoreType.DMA((2,)),
                        pltpu.SemaphoreType.DMA((2,))],
    )
    return pl.pallas_call(
        functools.partial(_moe_kernel, alpha=alpha, tm=tm, d=d, n_tiles=n_tiles),
        grid_spec=grid_spec,
        out_shape=jax.ShapeDtypeStruct((t, d), F32),
        compiler_params=_cparams(("arbitrary",)),
        name="moe_ln",
    )(src, tg, nvalid, x1e, wg_bf, wu_bf, wd_bf, ln_g, ln_b)


def kernel(x, w_in, conv_w, conv_b, w_rg_a, b_rg_a, w_rg_i, b_rg_i, lru_lambda, g_attn_norm, g_rec_norm, w_out,
           ln1_g, ln1_b, w_router_group, b_router_group, w_router_expert, b_router_expert, w_gate, w_up, w_down,
           ln2_g, ln2_b):
    b, s, d = x.shape
    depth = w_in.shape[0]
    rw = conv_w.shape[2]
    aw = (w_in.shape[2] - 2 * rw) // 3
    alpha = (2 * depth) ** 0.25
    tables = _rope_tables(s)
    row = lambda v: v.reshape(1, -1)

    wg_bf, wu_bf, wd_bf = w_gate.astype(BF16), w_up.astype(BF16), w_down.astype(BF16)
    x2 = x.reshape(b * s, d)
    for l in range(depth):
        q, k, v, rec = _in_proj(x2, w_in, l, tables, s, aw)
        attn = _moba_attention(q.reshape(b, s, aw), k.reshape(b, s, aw), v.reshape(b, s, aw))
        w_gates = jnp.concatenate([_block_diag(w_rg_a[l]), _block_diag(w_rg_i[l])], axis=1).astype(BF16)
        rec_out = _rglru(rec.reshape(b, s, 2 * rw), conv_w[l], row(conv_b[l]), w_gates,
                         row(b_rg_a[l]), row(b_rg_i[l]), row(lru_lambda[l]))
        n_route = N_GROUPS + N_GROUPS * EXPERTS_PER_GROUP
        w_router = jnp.concatenate([w_router_group[l], w_router_expert[l]], axis=1)
        w_router = jnp.pad(w_router, ((0, 0), (0, LANES - n_route))).astype(BF16)
        b_router = jnp.pad(jnp.concatenate([b_router_group[l], b_router_expert[l]]), (0, LANES - n_route))
        x1e, gidx = _out_proj(attn.reshape(b * s, aw), rec_out.reshape(b * s, rw), x2,
                        row(g_attn_norm[l]), row(g_rec_norm[l]), w_out, l,
                        row(ln1_g[l]), row(ln1_b[l]), w_router, row(b_router), alpha)
        x2 = _moe(x1e, gidx[0].astype(jnp.int32), b * s, wg_bf, wu_bf, wd_bf, l,
                  row(ln2_g[l]), row(ln2_b[l]), alpha, d)
    return x2.reshape(b, s, d)
```

```python
import functools

import jax
import jax.numpy as jnp
from jax import lax
from jax.experimental import pallas as pl
from jax.experimental.pallas import tpu as pltpu

F32 = jnp.float32
BF16 = jnp.bfloat16

HEAD_DIM = 64
ROT_DIM = HEAD_DIM // 4
ROPE_THETA = 500000.0
MOBA_BLOCK = 256
MOBA_TOPK = 3
CONV_WIDTH = 4
LRU_C = 8.0
N_GROUPS = 4
EXPERTS_PER_GROUP = 4
LN_EPS = 1e-5
RMS_EPS = 1e-6

LANES = 128
SUBLANES = 8
NEG_BIG = -1e30
LOG2E = 1.4426950408889634
Q_SCALE = HEAD_DIM ** -0.5 * LOG2E
AHEAD, BEHIND = 3, 1
VT_ROWS = HEAD_DIM + 16
VMEM_LIMIT = 56 * 1024 * 1024
ROW_TILE = 512
OUT_TILE = 1024


def _cparams(sem):
    return pltpu.CompilerParams(dimension_semantics=sem, vmem_limit_bytes=VMEM_LIMIT)


def _inproj_kernel(x_ref, w_ref, c_ref, s1_ref, s2_ref, q_ref, k_ref, v_ref, rec_ref, *, aw):
    xb = x_ref[...].astype(BF16)
    cos, sin_lo, sin_hi = c_ref[...], s1_ref[...], s2_ref[...]

    def rope(t):
        outs = []
        for c in range(aw // LANES):
            tc = t[:, c * LANES:(c + 1) * LANES]
            outs.append(tc * cos + pltpu.roll(tc, 8, 1) * sin_lo + pltpu.roll(tc, LANES - 8, 1) * sin_hi)
        return jnp.concatenate(outs, axis=1)

    q = jnp.dot(xb, w_ref[:, 0:aw].astype(BF16), preferred_element_type=F32)
    q_ref[...] = (rope(q) * Q_SCALE).astype(BF16)
    k = jnp.dot(xb, w_ref[:, aw:2 * aw].astype(BF16), preferred_element_type=F32)
    k_ref[...] = rope(k).astype(BF16)
    v = jnp.dot(xb, w_ref[:, 2 * aw:3 * aw].astype(BF16), preferred_element_type=F32)
    v_ref[...] = v.astype(BF16)
    rec_ref[...] = jnp.dot(xb, w_ref[:, 3 * aw:].astype(BF16), preferred_element_type=F32)


def _rope_tables(seq):
    half = ROT_DIM // 2
    inv_freq = ROPE_THETA ** (-jnp.arange(half, dtype=F32) * 2.0 / ROT_DIM)
    ang = jnp.arange(seq).astype(F32)[:, None] * inv_freq[None, :]
    cos, sin = jnp.cos(ang), jnp.sin(ang)
    zeros = jnp.zeros((seq, HEAD_DIM - ROT_DIM), F32)
    zh = jnp.zeros((seq, half), F32)
    c64 = jnp.concatenate([cos, cos, zeros + 1.0], axis=1)
    lo64 = jnp.concatenate([zh, sin, zeros], axis=1)
    hi64 = jnp.concatenate([-sin, zh, zeros], axis=1)
    rep = LANES // HEAD_DIM
    return jnp.tile(c64, (1, rep)), jnp.tile(lo64, (1, rep)), jnp.tile(hi64, (1, rep))


def _in_proj(x2, w_all, layer, tables, seq, aw):
    t, d = x2.shape
    n = w_all.shape[2]
    tm = min(ROW_TILE, seq)
    nseq = seq // tm
    tab_spec = pl.BlockSpec((tm, LANES), lambda i: (i % nseq, 0))
    return pl.pallas_call(
        functools.partial(_inproj_kernel, aw=aw),
        grid=(t // tm,),
        in_specs=[pl.BlockSpec((tm, d), lambda i: (i, 0)),
                  pl.BlockSpec((None, d, n), lambda i: (layer, 0, 0)),
                  tab_spec, tab_spec, tab_spec],
        out_specs=[pl.BlockSpec((tm, aw), lambda i: (i, 0)),
                   pl.BlockSpec((tm, aw), lambda i: (i, 0)),
                   pl.BlockSpec((tm, aw), lambda i: (i, 0)),
                   pl.BlockSpec((tm, n - 3 * aw), lambda i: (i, 0))],
        out_shape=[jax.ShapeDtypeStruct((t, aw), BF16),
                   jax.ShapeDtypeStruct((t, aw), BF16),
                   jax.ShapeDtypeStruct((t, aw), BF16),
                   jax.ShapeDtypeStruct((t, n - 3 * aw), F32)],
        compiler_params=_cparams(("parallel",)),
        name="in_proj",
    )(x2, w_all, *tables)


def _attn_kernel(q_ref, k_ref, v_ref, o_ref, vt_ref, km_ref, bias_ref, m_ref, acc_ref, *, nb, npair):
    blk = MOBA_BLOCK
    i = pl.program_id(1)
    lane = lax.broadcasted_iota(jnp.int32, (1, LANES), 1)
    head_lanes = (lane < HEAD_DIM, lane >= HEAD_DIM)
    nt = (((1,), (1,)), ((), ()))
    heads = [(p, h) for p in range(npair) for h in range(2)]
    pair_lanes = lambda p: slice(p * LANES, (p + 1) * LANES)

    @pl.when(i == 0)
    def _prepare():
        def prep(j, _):
            off = pl.multiple_of(j * blk, blk)
            for p in range(npair):
                kb = k_ref[0, pl.ds(off, blk), pair_lanes(p)].astype(F32)
                km = jnp.sum(kb, axis=0, keepdims=True) * (1.0 / blk)
                km_ref[2 * p, pl.ds(j, 1), :] = jnp.where(head_lanes[0], km, 0.0)
                km_ref[2 * p + 1, pl.ds(j, 1), :] = jnp.where(head_lanes[1], km, 0.0)
                vt = v_ref[0, pl.ds(off, blk), pair_lanes(p)].astype(F32).T
                vt_ref[2 * p, 0:HEAD_DIM, pl.ds(off, blk)] = vt[0:HEAD_DIM].astype(BF16)
                vt_ref[2 * p + 1, 0:HEAD_DIM, pl.ds(off, blk)] = vt[HEAD_DIM:].astype(BF16)
            return 0

        lax.fori_loop(0, nb, prep, 0)
        vt_ref[:, HEAD_DIM:, :] = jnp.ones((2 * npair, VT_ROWS - HEAD_DIM, nb * blk), BF16)

    qh = []
    for p, h in heads:
        q_pair = q_ref[0, :, pair_lanes(p)]
        qh.append(jnp.where(head_lanes[h], q_pair, jnp.zeros_like(q_pair)))

    def block_gate(n):
        p, h = heads[n]
        n_iota = lax.broadcasted_iota(jnp.int32, (nb, blk), 0)
        valid = n_iota < i
        g = lax.dot_general(km_ref[n], q_ref[0, :, pair_lanes(p)].astype(F32), nt, preferred_element_type=F32)
        g = jnp.where(valid, g, -jnp.inf)
        bias = jnp.full((nb, blk), NEG_BIG, F32)
        for _ in range(min(MOBA_TOPK, nb)):
            top = jnp.max(g, axis=0, keepdims=True)
            first = jnp.min(jnp.where(g == top, n_iota, nb), axis=0, keepdims=True)
            hit = n_iota == first
            bias = jnp.where(hit & valid, 0.0, bias)
            g = jnp.where(hit, -jnp.inf, g)
        bias_ref[n] = bias

    off_i = pl.multiple_of(i * blk, blk)
    kpos = lax.broadcasted_iota(jnp.int32, (blk, blk), 0)
    qpos = lax.broadcasted_iota(jnp.int32, (blk, blk), 1)
    causal = kpos <= qpos

    def scores(n, off, rows):
        kb = k_ref[0, pl.ds(off, rows), pair_lanes(heads[n][0])]
        return lax.dot_general(kb, qh[n], nt, preferred_element_type=F32)

    def interleave(stage_scores, stage_probs, stage_values, count):
        s_q = {n: stage_scores(n) for n in range(min(AHEAD, count))}
        p_q = {}
        for t in range(count + BEHIND):
            if t + AHEAD < count:
                s_q[t + AHEAD] = stage_scores(t + AHEAD)
            if t < count:
                p_q[t] = stage_probs(t, s_q.pop(t))
            if t >= BEHIND:
                stage_values(t - BEHIND, *p_q.pop(t - BEHIND))

    def own_probs(n, s):
        s = jnp.where(causal, s, NEG_BIG)
        m0 = jnp.max(s, axis=0, keepdims=True)
        m_ref[n] = m0
        return (jnp.exp2(s - m0).astype(BF16),)

    def own_values(n, pr):
        acc_ref[n] = jnp.dot(vt_ref[n, :, pl.ds(off_i, blk)], pr, preferred_element_type=F32)
        block_gate(n)

    nh = len(heads)
    interleave(lambda n: scores(n, off_i, blk), own_probs, own_values, nh)

    def run_pairs(first_pair, n_pairs):
        def j0_of(t):
            return 2 * (first_pair + t // nh)

        def off_of(t):
            return pl.multiple_of(j0_of(t) * blk, 2 * blk)

        def probs(t, s):
            n, j0 = t % nh, j0_of(t)
            b0 = bias_ref[n, pl.ds(j0, 1), :]
            b1 = bias_ref[n, pl.ds(j0 + 1, 1), :]
            s0, s1 = s[0:blk], s[blk:]
            mj = jnp.maximum(jnp.max(s0, axis=0, keepdims=True) + b0, jnp.max(s1, axis=0, keepdims=True) + b1)
            m_run = m_ref[n]
            m_new = jnp.maximum(m_run, mj)
            m_ref[n] = m_new
            alpha = jnp.exp2(m_run - m_new)
            p0 = jnp.exp2(s0 - (m_new - b0)).astype(BF16)
            p1 = jnp.exp2(s1 - (m_new - b1)).astype(BF16)
            return alpha, p0, p1

        def accumulate(t, alpha, p0, p1):
            n, off = t % nh, off_of(t)
            pv = (jnp.dot(vt_ref[n, :, pl.ds(off, blk)], p0, preferred_element_type=F32)
                  + jnp.dot(vt_ref[n, :, pl.ds(off + blk, blk)], p1, preferred_element_type=F32))
            acc_ref[n] = alpha * acc_ref[n] + pv

        interleave(lambda t: scores(t % nh, off_of(t), 2 * blk), probs, accumulate, n_pairs * nh)

    n_pairs = (i + 1) // 2

    def quad_body(jj, _):
        run_pairs(4 * jj, 4)
        return 0

    lax.fori_loop(0, n_pairs // 4, quad_body, 0)

    @pl.when((n_pairs // 2) % 2 == 1)
    def _pair_of_pairs():
        run_pairs(4 * (n_pairs // 4), 2)

    @pl.when(n_pairs % 2 == 1)
    def _last_pair():
        run_pairs(n_pairs - 1, 1)

    for p in range(npair):
        outs = []
        for h in range(2):
            acc = acc_ref[2 * p + h]
            outs.append(acc[0:HEAD_DIM] / acc[HEAD_DIM:HEAD_DIM + 1])
        o_ref[0, :, pair_lanes(p)] = jnp.concatenate(outs, axis=0).T.astype(o_ref.dtype)


def _moba_attention(q, k, v):
    b, s, aw = q.shape
    nb = s // MOBA_BLOCK
    npair = aw // LANES
    nh = 2 * npair
    return pl.pallas_call(
        functools.partial(_attn_kernel, nb=nb, npair=npair),
        grid=(b, nb),
        in_specs=[pl.BlockSpec((1, MOBA_BLOCK, aw), lambda b_, i: (b_, i, 0)),
                  pl.BlockSpec((1, s, aw), lambda b_, i: (b_, 0, 0)),
                  pl.BlockSpec((1, s, aw), lambda b_, i: (b_, 0, 0))],
        out_specs=pl.BlockSpec((1, MOBA_BLOCK, aw), lambda b_, i: (b_, i, 0)),
        out_shape=jax.ShapeDtypeStruct((b, s, aw), BF16),
        scratch_shapes=[pltpu.VMEM((nh, VT_ROWS, s), BF16),
                        pltpu.VMEM((nh, nb, LANES), F32),
                        pltpu.VMEM((nh, nb, MOBA_BLOCK), F32),
                        pltpu.VMEM((nh, 1, MOBA_BLOCK), F32),
                        pltpu.VMEM((nh, VT_ROWS, MOBA_BLOCK), F32)],
        compiler_params=_cparams(("parallel", "arbitrary")),
        name="moba_attn",
    )(q, k, v)


def _rglru_kernel(rec_ref, cw_ref, cb_ref, wg_ref, ba_ref, bi_ref, lam_ref, o_ref,
                  tail_ref, h_ref, a_ref, u_ref, hin_ref, xs_ref, *, rw):
    ts = rec_ref.shape[1]
    t_idx = pl.program_id(1)

    @pl.when(t_idx == 0)
    def _reset():
        tail_ref[...] = jnp.zeros_like(tail_ref)
        h_ref[...] = jnp.zeros_like(h_ref)

    xr = rec_ref[0, :, 0:rw]
    xs_ref[0:SUBLANES, :] = tail_ref[...]
    xs_ref[SUBLANES:, :] = xr
    tail_ref[...] = xr[ts - SUBLANES:, :]
    xc = xr * cw_ref[CONV_WIDTH - 1:CONV_WIDTH, :] + cb_ref[...]
    for d in range(1, CONV_WIDTH):
        xc = xc + xs_ref[SUBLANES - d:SUBLANES - d + ts, :] * cw_ref[CONV_WIDTH - 1 - d:CONV_WIDTH - d, :]

    pre = jnp.dot(xc.astype(BF16), wg_ref[...], preferred_element_type=F32)
    sigmoid = lambda v: 0.5 * jnp.tanh(0.5 * v) + 0.5
    r = sigmoid(pre[:, 0:rw] + ba_ref[...])
    gi = sigmoid(pre[:, rw:] + bi_ref[...])
    lam = lam_ref[...]
    softplus_neg = jnp.maximum(-lam, 0.0) + jnp.log1p(jnp.exp(-jnp.abs(lam)))
    log_a = (-LRU_C * r) * softplus_neg
    a = jnp.exp(log_a)
    z = -jnp.tanh(log_a) * (a * a + 1.0)
    root = jnp.where(z > 0.0, z * lax.rsqrt(z), 0.0)
    u = root * (gi * xc)

    ng = ts // SUBLANES
    sub = lax.broadcasted_iota(jnp.int32, (ts, 1), 0) % SUBLANES
    d = 1
    while d < SUBLANES:
        live = sub >= d
        u = jnp.where(live, a * pltpu.roll(u, d, 0) + u, u)
        a = jnp.where(live, a * pltpu.roll(a, d, 0), a)
        d *= 2
    nplane = rw // LANES
    last_rows = pl.ds(SUBLANES - 1, ng, stride=SUBLANES)
    for c in range(nplane):
        a_ref[c] = a[:, c * LANES:(c + 1) * LANES]
        u_ref[c] = u[:, c * LANES:(c + 1) * LANES]
    ag = jnp.concatenate([a_ref[c, last_rows, :] for c in range(nplane)], axis=1)
    ug = jnp.concatenate([u_ref[c, last_rows, :] for c in range(nplane)], axis=1)
    grow = lax.broadcasted_iota(jnp.int32, (ng, 1), 0)
    d = 1
    while d < ng:
        live = grow >= d
        ug = jnp.where(live, ag * pltpu.roll(ug, d, 0) + ug, ug)
        ag = jnp.where(live, ag * pltpu.roll(ag, d, 0), ag)
        d *= 2
    h0 = h_ref[...]
    hg = ag * h0 + ug
    hin_ref[...] = jnp.where(grow >= 1, pltpu.roll(hg, 1, 0), h0)
    h_ref[...] = hg[ng - 1:, :]
    for g2 in range(0, ng, 2):
        hs = []
        for g in (g2, g2 + 1):
            rows = slice(g * SUBLANES, (g + 1) * SUBLANES)
            a_g = jnp.concatenate([a_ref[c, rows, :] for c in range(nplane)], axis=1)
            u_g = jnp.concatenate([u_ref[c, rows, :] for c in range(nplane)], axis=1)
            hs.append(a_g * hin_ref[g:g + 1, :] + u_g)
        rows = slice(g2 * SUBLANES, (g2 + 2) * SUBLANES)
        gated = jnp.concatenate(hs, axis=0) * jax.nn.gelu(rec_ref[0, rows, rw:])
        o_ref[0, rows, :] = gated.astype(o_ref.dtype)


def _rglru(rec, conv_w, conv_b, w_gates_bf, b_a, b_i, lam):
    b, s, two_rw = rec.shape
    rw = two_rw // 2
    ts = min(ROW_TILE, s)
    full = lambda shape: pl.BlockSpec(shape, lambda b_, t: (0,) * len(shape))
    return pl.pallas_call(
        functools.partial(_rglru_kernel, rw=rw),
        grid=(b, s // ts),
        in_specs=[pl.BlockSpec((1, ts, two_rw), lambda b_, t: (b_, t, 0)),
                  full((CONV_WIDTH, rw)), full((1, rw)), full((rw, two_rw)),
                  full((1, rw)), full((1, rw)), full((1, rw))],
        out_specs=pl.BlockSpec((1, ts, rw), lambda b_, t: (b_, t, 0)),
        out_shape=jax.ShapeDtypeStruct((b, s, rw), BF16),
        scratch_shapes=[pltpu.VMEM((SUBLANES, rw), F32), pltpu.VMEM((1, rw), F32),
                        pltpu.VMEM((rw // LANES, ts, LANES), F32), pltpu.VMEM((rw // LANES, ts, LANES), F32),
                        pltpu.VMEM((ts // SUBLANES, rw), F32), pltpu.VMEM((ts + SUBLANES, rw), F32)],
        compiler_params=_cparams(("parallel", "arbitrary")),
        name="rglru",
    )(rec, conv_w, conv_b, w_gates_bf, b_a, b_i, lam)


def _block_diag(w):
    n, c, d = w.shape
    eye = jnp.eye(n, dtype=w.dtype)
    return (w[:, :, None, :] * eye[:, None, :, None]).reshape(n * c, n * d)


def _layer_norm(z, g, b):
    mu = jnp.mean(z, axis=-1, keepdims=True)
    zc = z - mu
    var = jnp.mean(zc * zc, axis=-1, keepdims=True)
    return zc * lax.rsqrt(var + LN_EPS) * g + b


def _rms(z, g):
    return z * lax.rsqrt(jnp.mean(z * z, axis=-1, keepdims=True) + RMS_EPS) * g


def _route(logits):
    lane = lax.broadcasted_iota(jnp.int32, logits.shape, 1).astype(F32)
    neg_inf = -jnp.inf
    first_lane = lambda mask: jnp.min(jnp.where(mask, lane, float(LANES)), axis=-1, keepdims=True)
    gl = jnp.where(lane < N_GROUPS, logits, neg_inf)
    gmax = jnp.max(gl, axis=-1, keepdims=True)
    gsum = jnp.sum(jnp.exp(gl - gmax), axis=-1, keepdims=True)
    g_p = 1.0 / gsum
    g_idx = first_lane(gl == gmax)
    lo = N_GROUPS + EXPERTS_PER_GROUP * g_idx
    emask = (lane >= lo) & (lane < lo + EXPERTS_PER_GROUP)
    el = jnp.where(emask, logits, neg_inf)
    emax = jnp.max(el, axis=-1, keepdims=True)
    ee = jnp.exp(el - emax)
    ep = ee / jnp.sum(ee, axis=-1, keepdims=True)
    e1 = jnp.max(ep, axis=-1, keepdims=True)
    i1 = first_lane(emask & (ep == e1))
    rest = emask & (lane != i1)
    ep2 = jnp.where(rest, ep, neg_inf)
    e2 = jnp.max(ep2, axis=-1, keepdims=True)
    i2 = first_lane(rest & (ep2 == e2))
    denom = e1 + e2
    w1 = g_p * e1 / denom
    w2 = g_p * e2 / denom
    comb = jnp.where(lane == i1, w1, 0.0) + jnp.where(lane == i2, w2, 0.0)
    return jnp.where(lane == 0, g_idx, comb)


def _outproj_kernel(attn_ref, rec_ref, x_ref, ga_ref, gr_ref, w_ref, g_ref, b_ref, wr_ref, br_ref,
                    x1_ref, gidx_ref, *, alpha, d, n_sub):
    sub = x_ref.shape[0] // n_sub
    rows = [slice(k * sub, (k + 1) * sub) for k in range(n_sub)]
    mixed = [jnp.concatenate([_rms(attn_ref[r, :].astype(F32), ga_ref[...]),
                              _rms(rec_ref[r, :].astype(F32), gr_ref[...])],
                             axis=-1).astype(BF16) for r in rows]
    w = w_ref[...].astype(BF16)
    h = [jnp.dot(m, w, preferred_element_type=F32) for m in mixed]
    logits = []
    for r, hk in zip(rows, h):
        x1 = _layer_norm(alpha * x_ref[r, :] + hk, g_ref[...], b_ref[...])
        x1_ref[r, 0:d] = x1
        logits.append(jnp.dot(x1.astype(BF16), wr_ref[...], preferred_element_type=F32) + br_ref[...])
    pick = ((lax.broadcasted_iota(jnp.int32, (SUBLANES, LANES), 0) == 0)
            & (lax.broadcasted_iota(jnp.int32, (SUBLANES, LANES), 1) == 0)).astype(BF16)
    for r, lg in zip(rows, logits):
        route = _route(lg)
        x1_ref[r, d:] = route
        moved = lax.dot_general(pick, route.astype(BF16), (((1,), (1,)), ((), ())), preferred_element_type=F32)
        gidx_ref[0:1, r] = moved[0:1, :]


def _out_proj(attn2, rec2, x2, g_a, g_r, w_out_all, layer, ln_g, ln_b, w_router_bf, b_router, alpha):
    t, d = x2.shape
    aw, rw = attn2.shape[1], rec2.shape[1]
    tm = min(OUT_TILE, t)
    n_sub = 2 if tm == OUT_TILE else 1
    row = lambda w: pl.BlockSpec((tm, w), lambda i: (i, 0))
    full = lambda shape: pl.BlockSpec(shape, lambda i: (0,) * len(shape))
    return pl.pallas_call(
        functools.partial(_outproj_kernel, alpha=alpha, d=d, n_sub=n_sub),
        grid=(t // tm,),
        in_specs=[row(aw), row(rw), row(d), full((1, aw)), full((1, rw)),
                  pl.BlockSpec((None, aw + rw, d), lambda i: (layer, 0, 0)),
                  full((1, d)), full((1, d)), full((d, LANES)), full((1, LANES))],
        out_specs=[row(d + LANES), pl.BlockSpec((1, tm), lambda i: (0, i))],
        out_shape=[jax.ShapeDtypeStruct((t, d + LANES), F32), jax.ShapeDtypeStruct((1, t), F32)],
        compiler_params=_cparams(("parallel",)),
        name="out_proj_ln_router",
    )(attn2, rec2, x2, g_a, g_r, w_out_all, ln_g, ln_b, w_router_bf, b_router)


def _dispatch_plan(gidx, tm):
    t = gidx.shape[0]
    assert t % tm == 0
    n_tiles = t // tm + N_GROUPS - 1
    counts = jnp.sum((gidx[:, None] == jnp.arange(N_GROUPS)[None, :]).astype(jnp.int32), axis=0)
    tiles_g = (counts + tm - 1) // tm
    tile_end = jnp.cumsum(tiles_g)
    tile_start = tile_end - tiles_g
    n_used = tile_end[-1]
    sorted_start = jnp.cumsum(counts) - counts
    order = jnp.argsort(gidx, stable=True).astype(jnp.int32)
    tile = jnp.arange(n_tiles, dtype=jnp.int32)
    tg = jnp.sum((tile_end[None, :] <= tile[:, None]).astype(jnp.int32), axis=1)
    used = tile < n_used
    tg = jnp.where(used, tg, tg[n_used - 1])
    row_off = (tile - tile_start[tg]) * tm
    nvalid = jnp.where(used, jnp.clip(counts[tg] - row_off, 0, tm), 0).astype(jnp.int32)
    r = jnp.arange(tm, dtype=jnp.int32)[None, :]
    pos = sorted_start[tg][:, None] + row_off[:, None] + jnp.minimum(r, nvalid[:, None] - 1)
    src = order[jnp.clip(pos, 0, t - 1)]
    return src.reshape(-1), tg.astype(jnp.int32), nvalid


def _moe_kernel(src_ref, tg_ref, nv_ref, x_hbm, wg_ref, wu_ref, wd_ref, g_ref, b_ref, out_hbm,
                xbuf0, xbuf1, obuf0, obuf1, gsem, ssem, *, alpha, tm, d, n_tiles):
    i = pl.program_id(0)
    nxt = jnp.minimum(i + 1, n_tiles - 1)
    prv = jnp.maximum(i - 1, 0)
    nv_prv = jnp.where(i > 0, nv_ref[prv], 0)
    pp = jnp.maximum(i - 2, 0)
    nv_pp = jnp.where(i > 1, nv_ref[pp], 0)
    prv_full = nv_prv == tm

    def gather_row(tile, r, xbuf, sem):
        tok = src_ref[tile * tm + r]
        return pltpu.make_async_copy(x_hbm.at[pl.ds(tok, 1), :], xbuf.at[pl.ds(r, 1), :], sem)

    def scatter_row(tile, r, obuf, sem):
        tok = src_ref[tile * tm + r]
        return pltpu.make_async_copy(obuf.at[pl.ds(r, 1), :], out_hbm.at[pl.ds(tok, 1), :], sem)

    gather_tile = lambda xbuf, sem: pltpu.make_async_copy(x_hbm.at[pl.ds(0, tm), :], xbuf.at[pl.ds(0, tm), :], sem)
    scatter_tile = lambda obuf, sem: pltpu.make_async_copy(obuf, out_hbm.at[pl.ds(0, tm), :], sem)

    def for_rows(n, fn, unroll):
        def body(r, _):
            fn(r)
            return 0
        lax.fori_loop(0, n, body, 0, unroll=unroll)

    def step(x_cur, x_nxt, o_cur, o_prv, g_cur, g_nxt, s_cur, s_prv):
        @pl.when(i == 0)
        def _prologue():
            for_rows(tm, lambda r: gather_row(0, r, x_cur, g_cur).start(), 8)
            x_cur[tm:, :] = jnp.zeros((SUBLANES, d + LANES), F32)
            x_nxt[tm:, :] = jnp.zeros((SUBLANES, d + LANES), F32)

        gather_tile(x_cur, g_cur).wait()
        x = x_cur[0:tm, 0:d]
        route = x_cur[0:tm, d:]
        xb = x.astype(BF16)
        lane = lax.broadcasted_iota(jnp.int32, route.shape, 1)
        base = N_GROUPS + EXPERTS_PER_GROUP * tg_ref[i]
        rows_per = tm // EXPERTS_PER_GROUP
        de = wg_ref.shape[2]
        def gate_up(e):
            for r in range(e * rows_per, (e + 1) * rows_per):
                gather_row(nxt, r, x_nxt, g_nxt).start()
            zero_row = x_nxt[tm:tm + 1, 0:de]
            gate = jnp.dot(xb, wg_ref[e], preferred_element_type=F32)
            up = jnp.dot(xb, wu_ref[e], preferred_element_type=F32)
            return gate, up, zero_row

        def activate(gate, up, zero_row):
            return (jax.nn.silu(gate) * up + zero_row).astype(BF16)

        def down(e, hidden):
            w_e = jnp.sum(jnp.where(lane == base + e, route, 0.0), axis=-1, keepdims=True)
            return w_e * jnp.dot(hidden, wd_ref[e], preferred_element_type=F32)

        n_exp = EXPERTS_PER_GROUP
        pre = {0: gate_up(0), 1: gate_up(1)}
        hid = {}
        y = None
        for t in range(n_exp + 1):
            if t + 2 < n_exp:
                pre[t + 2] = gate_up(t + 2)
            if t < n_exp:
                hid[t] = activate(*pre.pop(t))
            if t >= 1:
                contrib = down(t - 1, hid.pop(t - 1))
                y = contrib if y is None else y + contrib
        z = alpha * x + y

        @pl.when(nv_pp == tm)
        def _drain_full():
            scatter_tile(o_cur, s_cur).wait()

        @pl.when(nv_pp < tm)
        def _drain_partial():
            for_rows(nv_pp, lambda r: scatter_row(pp, r, o_cur, s_cur).wait(), 1)

        @pl.when(prv_full)
        def _finish_after_full_tile():
            o_cur[...] = _layer_norm(z, g_ref[...], b_ref[...])
            for r in range(tm):
                scatter_row(prv, r, o_prv, s_prv).start()

        @pl.when(jnp.logical_not(prv_full))
        def _finish_after_partial_tile():
            o_cur[...] = _layer_norm(z, g_ref[...], b_ref[...])
            for_rows(nv_prv, lambda r: scatter_row(prv, r, o_prv, s_prv).start(), 1)

        @pl.when(i == n_tiles - 1)
        def _epilogue():
            gather_tile(x_nxt, g_nxt).wait()
            for_rows(nv_prv, lambda r: scatter_row(prv, r, o_prv, s_prv).wait(), 1)
            for_rows(nv_ref[i], lambda r: scatter_row(i, r, o_cur, s_cur).start(), 1)
            for_rows(nv_ref[i], lambda r: scatter_row(i, r, o_cur, s_cur).wait(), 1)

    @pl.when(i % 2 == 0)
    def _even():
        step(xbuf0, xbuf1, obuf0, obuf1, gsem.at[0], gsem.at[1], ssem.at[0], ssem.at[1])

    @pl.when(i % 2 == 1)
    def _odd():
        step(xbuf1, xbuf0, obuf1, obuf0, gsem.at[1], gsem.at[0], ssem.at[1], ssem.at[0])


def _moe(x1e, gidx, t, wg_bf, wu_bf, wd_bf, layer, ln_g, ln_b, alpha, d):
    de = wg_bf.shape[3]
    tm = min(ROW_TILE, t)
    src, tg, nvalid = _dispatch_plan(gidx, tm)
    n_tiles = tg.shape[0]
    epg = EXPERTS_PER_GROUP
    full = lambda shape: pl.BlockSpec(shape, lambda i, *_: (0,) * len(shape))
    grid_spec = pltpu.PrefetchScalarGridSpec(
        num_scalar_prefetch=3,
        grid=(n_tiles,),
        in_specs=[pl.BlockSpec(memory_space=pl.ANY),
                  pl.BlockSpec((None, epg, d, de), lambda i, src_, tg_, nv_: (layer, tg_[i], 0, 0)),
                  pl.BlockSpec((None, epg, d, de), lambda i, src_, tg_, nv_: (layer, tg_[i], 0, 0)),
                  pl.BlockSpec((None, epg, de, d), lambda i, src_, tg_, nv_: (layer, tg_[i], 0, 0)),
                  full((1, d)), full((1, d))],
        out_specs=pl.BlockSpec(memory_space=pl.ANY),
        scratch_shapes=[pltpu.VMEM((tm + SUBLANES, d + LANES), F32), pltpu.VMEM((tm + SUBLANES, d + LANES), F32),
                        pltpu.VMEM((tm, d), F32), pltpu.VMEM((tm, d), F32),
                        pltpu.SemaphoreType.DMA((2,)),
                        pltpu.SemaphoreType.DMA((2,))],
    )
    return pl.pallas_call(
        functools.partial(_moe_kernel, alpha=alpha, tm=tm, d=d, n_tiles=n_tiles),
        grid_spec=grid_spec,
        out_shape=jax.ShapeDtypeStruct((t, d), F32),
        compiler_params=_cparams(("arbitrary",)),
        name="moe_ln",
    )(src, tg, nvalid, x1e, wg_bf, wu_bf, wd_bf, ln_g, ln_b)


def kernel(x, w_in, conv_w, conv_b, w_rg_a, b_rg_a, w_rg_i, b_rg_i, lru_lambda, g_attn_norm, g_rec_norm, w_out,
           ln1_g, ln1_b, w_router_group, b_router_group, w_router_expert, b_router_expert, w_gate, w_up, w_down,
           ln2_g, ln2_b):
    b, s, d = x.shape
    depth = w_in.shape[0]
    rw = conv_w.shape[2]
    aw = (w_in.shape[2] - 2 * rw) // 3
    alpha = (2 * depth) ** 0.25
    tables = _rope_tables(s)
    row = lambda v: v.reshape(1, -1)

    wg_bf, wu_bf, wd_bf = w_gate.astype(BF16), w_up.astype(BF16), w_down.astype(BF16)
    x2 = x.reshape(b * s, d)
    for l in range(depth):
        q, k, v, rec = _in_proj(x2, w_in, l, tables, s, aw)
        attn = _moba_attention(q.reshape(b, s, aw), k.reshape(b, s, aw), v.reshape(b, s, aw))
        w_gates = jnp.concatenate([_block_diag(w_rg_a[l]), _block_diag(w_rg_i[l])], axis=1).astype(BF16)
        rec_out = _rglru(rec.reshape(b, s, 2 * rw), conv_w[l], row(conv_b[l]), w_gates,
                         row(b_rg_a[l]), row(b_rg_i[l]), row(lru_lambda[l]))
        n_route = N_GROUPS + N_GROUPS * EXPERTS_PER_GROUP
        w_router = jnp.concatenate([w_router_group[l], w_router_expert[l]], axis=1)
        w_router = jnp.pad(w_router, ((0, 0), (0, LANES - n_route))).astype(BF16)
        b_router = jnp.pad(jnp.concatenate([b_router_group[l], b_router_expert[l]]), (0, LANES - n_route))
        x1e, gidx = _out_proj(attn.reshape(b * s, aw), rec_out.reshape(b * s, rw), x2,
                        row(g_attn_norm[l]), row(g_rec_norm[l]), w_out, l,
                        row(ln1_g[l]), row(ln1_b[l]), w_router, row(b_router), alpha)
        x2 = _moe(x1e, gidx[0].astype(jnp.int32), b * s, wg_bf, wu_bf, wd_bf, l,
                  row(ln2_g[l]), row(ln2_b[l]), alpha, d)
    return x2.reshape(b, s, d)
```

```python
import functools

import jax
import jax.numpy as jnp
from jax import lax
from jax.experimental import pallas as pl
from jax.experimental.pallas import tpu as pltpu

F32 = jnp.float32
BF16 = jnp.bfloat16

HEAD_DIM = 64
ROT_DIM = HEAD_DIM // 4
ROPE_THETA = 500000.0
MOBA_BLOCK = 256
MOBA_TOPK = 3
CONV_WIDTH = 4
LRU_C = 8.0
N_GROUPS = 4
EXPERTS_PER_GROUP = 4
LN_EPS = 1e-5
RMS_EPS = 1e-6

LANES = 128
SUBLANES = 8
NEG_BIG = -1e30
LOG2E = 1.4426950408889634
Q_SCALE = HEAD_DIM ** -0.5 * LOG2E
AHEAD, BEHIND = 3, 1
VT_ROWS = HEAD_DIM + 16
VMEM_LIMIT = 56 * 1024 * 1024
ROW_TILE = 512
OUT_TILE = 1024


def _cparams(sem):
    return pltpu.CompilerParams(dimension_semantics=sem, vmem_limit_bytes=VMEM_LIMIT)


def _inproj_kernel(x_ref, w_ref, c_ref, s1_ref, s2_ref, q_ref, k_ref, v_ref, rec_ref, *, aw):
    xb = x_ref[...].astype(BF16)
    cos, sin_lo, sin_hi = c_ref[...], s1_ref[...], s2_ref[...]

    def rope(t):
        outs = []
        for c in range(aw // LANES):
            tc = t[:, c * LANES:(c + 1) * LANES]
            outs.append(tc * cos + pltpu.roll(tc, 8, 1) * sin_lo + pltpu.roll(tc, LANES - 8, 1) * sin_hi)
        return jnp.concatenate(outs, axis=1)

    q = jnp.dot(xb, w_ref[:, 0:aw].astype(BF16), preferred_element_type=F32)
    q_ref[...] = (rope(q) * Q_SCALE).astype(BF16)
    k = jnp.dot(xb, w_ref[:, aw:2 * aw].astype(BF16), preferred_element_type=F32)
    k_ref[...] = rope(k).astype(BF16)
    v = jnp.dot(xb, w_ref[:, 2 * aw:3 * aw].astype(BF16), preferred_element_type=F32)
    v_ref[...] = v.astype(BF16)
    rec_ref[...] = jnp.dot(xb, w_ref[:, 3 * aw:].astype(BF16), preferred_element_type=F32)


def _rope_tables(seq):
    half = ROT_DIM // 2
    inv_freq = ROPE_THETA ** (-jnp.arange(half, dtype=F32) * 2.0 / ROT_DIM)
    ang = jnp.arange(seq).astype(F32)[:, None] * inv_freq[None, :]
    cos, sin = jnp.cos(ang), jnp.sin(ang)
    zeros = jnp.zeros((seq, HEAD_DIM - ROT_DIM), F32)
    zh = jnp.zeros((seq, half), F32)
    c64 = jnp.concatenate([cos, cos, zeros + 1.0], axis=1)
    lo64 = jnp.concatenate([zh, sin, zeros], axis=1)
    hi64 = jnp.concatenate([-sin, zh, zeros], axis=1)
    rep = LANES // HEAD_DIM
    return jnp.tile(c64, (1, rep)), jnp.tile(lo64, (1, rep)), jnp.tile(hi64, (1, rep))


def _in_proj(x2, w_all, layer, tables, seq, aw):
    t, d = x2.shape
    n = w_all.shape[2]
    tm = min(ROW_TILE, seq)
    nseq = seq // tm
    tab_spec = pl.BlockSpec((tm, LANES), lambda i: (i % nseq, 0))
    return pl.pallas_call(
        functools.partial(_inproj_kernel, aw=aw),
        grid=(t // tm,),
        in_specs=[pl.BlockSpec((tm, d), lambda i: (i, 0)),
                  pl.BlockSpec((None, d, n), lambda i: (layer, 0, 0)),
                  tab_spec, tab_spec, tab_spec],
        out_specs=[pl.BlockSpec((tm, aw), lambda i: (i, 0)),
                   pl.BlockSpec((tm, aw), lambda i: (i, 0)),
                   pl.BlockSpec((tm, aw), lambda i: (i, 0)),
                   pl.BlockSpec((tm, n - 3 * aw), lambda i: (i, 0))],
        out_shape=[jax.ShapeDtypeStruct((t, aw), BF16),
                   jax.ShapeDtypeStruct((t, aw), BF16),
                   jax.ShapeDtypeStruct((t, aw), BF16),
                   jax.ShapeDtypeStruct((t, n - 3 * aw), F32)],
        compiler_params=_cparams(("parallel",)),
        name="in_proj",
    )(x2, w_all, *tables)


def _attn_kernel(q_ref, k_ref, v_ref, o_ref, vt_ref, km_ref, bias_ref, m_ref, acc_ref, *, nb, npair):
    blk = MOBA_BLOCK
    i = pl.program_id(1)
    lane = lax.broadcasted_iota(jnp.int32, (1, LANES), 1)
    head_lanes = (lane < HEAD_DIM, lane >= HEAD_DIM)
    nt = (((1,), (1,)), ((), ()))
    heads = [(p, h) for p in range(npair) for h in range(2)]
    pair_lanes = lambda p: slice(p * LANES, (p + 1) * LANES)

    @pl.when(i == 0)
    def _prepare():
        def prep(j, _):
            off = pl.multiple_of(j * blk, blk)
            for p in range(npair):
                kb = k_ref[0, pl.ds(off, blk), pair_lanes(p)].astype(F32)
                km = jnp.sum(kb, axis=0, keepdims=True) * (1.0 / blk)
                km_ref[2 * p, pl.ds(j, 1), :] = jnp.where(head_lanes[0], km, 0.0)
                km_ref[2 * p + 1, pl.ds(j, 1), :] = jnp.where(head_lanes[1], km, 0.0)
                vt = v_ref[0, pl.ds(off, blk), pair_lanes(p)].astype(F32).T
                vt_ref[2 * p, 0:HEAD_DIM, pl.ds(off, blk)] = vt[0:HEAD_DIM].astype(BF16)
                vt_ref[2 * p + 1, 0:HEAD_DIM, pl.ds(off, blk)] = vt[HEAD_DIM:].astype(BF16)
            return 0

        lax.fori_loop(0, nb, prep, 0)
        vt_ref[:, HEAD_DIM:, :] = jnp.ones((2 * npair, VT_ROWS - HEAD_DIM, nb * blk), BF16)

    qh = []
    for p, h in heads:
        q_pair = q_ref[0, :, pair_lanes(p)]
        qh.append(jnp.where(head_lanes[h], q_pair, jnp.zeros_like(q_pair)))

    def block_gate(n):
        p, h = heads[n]
        n_iota = lax.broadcasted_iota(jnp.int32, (nb, blk), 0)
        valid = n_iota < i
        g = lax.dot_general(km_ref[n], q_ref[0, :, pair_lanes(p)].astype(F32), nt, preferred_element_type=F32)
        g = jnp.where(valid, g, -jnp.inf)
        bias = jnp.full((nb, blk), NEG_BIG, F32)
        for _ in range(min(MOBA_TOPK, nb)):
            top = jnp.max(g, axis=0, keepdims=True)
            first = jnp.min(jnp.where(g == top, n_iota, nb), axis=0, keepdims=True)
            hit = n_iota == first
            bias = jnp.where(hit & valid, 0.0, bias)
            g = jnp.where(hit, -jnp.inf, g)
        bias_ref[n] = bias

    off_i = pl.multiple_of(i * blk, blk)
    kpos = lax.broadcasted_iota(jnp.int32, (blk, blk), 0)
    qpos = lax.broadcasted_iota(jnp.int32, (blk, blk), 1)
    causal = kpos <= qpos

    def scores(n, off, rows):
        kb = k_ref[0, pl.ds(off, rows), pair_lanes(heads[n][0])]
        return lax.dot_general(kb, qh[n], nt, preferred_element_type=F32)

    def interleave(stage_scores, stage_probs, stage_values, count):
        s_q = {n: stage_scores(n) for n in range(min(AHEAD, count))}
        p_q = {}
        for t in range(count + BEHIND):
            if t + AHEAD < count:
                s_q[t + AHEAD] = stage_scores(t + AHEAD)
            if t < count:
                p_q[t] = stage_probs(t, s_q.pop(t))
            if t >= BEHIND:
                stage_values(t - BEHIND, *p_q.pop(t - BEHIND))

    def own_probs(n, s):
        s = jnp.where(causal, s, NEG_BIG)
        m0 = jnp.max(s, axis=0, keepdims=True)
        m_ref[n] = m0
        return (jnp.exp2(s - m0).astype(BF16),)

    def own_values(n, pr):
        acc_ref[n] = jnp.dot(vt_ref[n, :, pl.ds(off_i, blk)], pr, preferred_element_type=F32)
        block_gate(n)

    nh = len(heads)
    interleave(lambda n: scores(n, off_i, blk), own_probs, own_values, nh)

    def run_pairs(first_pair, n_pairs):
        def j0_of(t):
            return 2 * (first_pair + t // nh)

        def off_of(t):
            return pl.multiple_of(j0_of(t) * blk, 2 * blk)

        def probs(t, s):
            n, j0 = t % nh, j0_of(t)
            b0 = bias_ref[n, pl.ds(j0, 1), :]
            b1 = bias_ref[n, pl.ds(j0 + 1, 1), :]
            s0, s1 = s[0:blk], s[blk:]
            mj = jnp.maximum(jnp.max(s0, axis=0, keepdims=True) + b0, jnp.max(s1, axis=0, keepdims=True) + b1)
            m_run = m_ref[n]
            m_new = jnp.maximum(m_run, mj)
            m_ref[n] = m_new
            alpha = jnp.exp2(m_run - m_new)
            p0 = jnp.exp2(s0 - (m_new - b0)).astype(BF16)
            p1 = jnp.exp2(s1 - (m_new - b1)).astype(BF16)
            return alpha, p0, p1

        def accumulate(t, alpha, p0, p1):
            n, off = t % nh, off_of(t)
            pv = (jnp.dot(vt_ref[n, :, pl.ds(off, blk)], p0, preferred_element_type=F32)
                  + jnp.dot(vt_ref[n, :, pl.ds(off + blk, blk)], p1, preferred_element_type=F32))
            acc_ref[n] = alpha * acc_ref[n] + pv

        interleave(lambda t: scores(t % nh, off_of(t), 2 * blk), probs, accumulate, n_pairs * nh)

    n_pairs = (i + 1) // 2

    def quad_body(jj, _):
        run_pairs(4 * jj, 4)
        return 0

    lax.fori_loop(0, n_pairs // 4, quad_body, 0)

    @pl.when((n_pairs // 2) % 2 == 1)
    def _pair_of_pairs():
        run_pairs(4 * (n_pairs // 4), 2)

    @pl.when(n_pairs % 2 == 1)
    def _last_pair():
        run_pairs(n_pairs - 1, 1)

    for p in range(npair):
        outs = []
        for h in range(2):
            acc = acc_ref[2 * p + h]
            outs.append(acc[0:HEAD_DIM] / acc[HEAD_DIM:HEAD_DIM + 1])
        o_ref[0, :, pair_lanes(p)] = jnp.concatenate(outs, axis=0).T.astype(o_ref.dtype)


def _moba_attention(q, k, v):
    b, s, aw = q.shape
    nb = s // MOBA_BLOCK
    npair = aw // LANES
    nh = 2 * npair
    return pl.pallas_call(
        functools.partial(_attn_kernel, nb=nb, npair=npair),
        grid=(b, nb),
        in_specs=[pl.BlockSpec((1, MOBA_BLOCK, aw), lambda b_, i: (b_, i, 0)),
                  pl.BlockSpec((1, s, aw), lambda b_, i: (b_, 0, 0)),
                  pl.BlockSpec((1, s, aw), lambda b_, i: (b_, 0, 0))],
        out_specs=pl.BlockSpec((1, MOBA_BLOCK, aw), lambda b_, i: (b_, i, 0)),
        out_shape=jax.ShapeDtypeStruct((b, s, aw), BF16),
        scratch_shapes=[pltpu.VMEM((nh, VT_ROWS, s), BF16),
                        pltpu.VMEM((nh, nb, LANES), F32),
                        pltpu.VMEM((nh, nb, MOBA_BLOCK), F32),
                        pltpu.VMEM((nh, 1, MOBA_BLOCK), F32),
                        pltpu.VMEM((nh, VT_ROWS, MOBA_BLOCK), F32)],
        compiler_params=_cparams(("parallel", "arbitrary")),
        name="moba_attn",
    )(q, k, v)


def _rglru_kernel(rec_ref, cw_ref, cb_ref, wg_ref, ba_ref, bi_ref, lam_ref, o_ref,
                  tail_ref, h_ref, a_ref, u_ref, hin_ref, xs_ref, *, rw):
    ts = rec_ref.shape[1]
    t_idx = pl.program_id(1)

    @pl.when(t_idx == 0)
    def _reset():
        tail_ref[...] = jnp.zeros_like(tail_ref)
        h_ref[...] = jnp.zeros_like(h_ref)

    xr = rec_ref[0, :, 0:rw]
    xs_ref[0:SUBLANES, :] = tail_ref[...]
    xs_ref[SUBLANES:, :] = xr
    tail_ref[...] = xr[ts - SUBLANES:, :]
    xc = xr * cw_ref[CONV_WIDTH - 1:CONV_WIDTH, :] + cb_ref[...]
    for d in range(1, CONV_WIDTH):
        xc = xc + xs_ref[SUBLANES - d:SUBLANES - d + ts, :] * cw_ref[CONV_WIDTH - 1 - d:CONV_WIDTH - d, :]

    pre = jnp.dot(xc.astype(BF16), wg_ref[...], preferred_element_type=F32)
    sigmoid = lambda v: 0.5 * jnp.tanh(0.5 * v) + 0.5
    r = sigmoid(pre[:, 0:rw] + ba_ref[...])
    gi = sigmoid(pre[:, rw:] + bi_ref[...])
    lam = lam_ref[...]
    softplus_neg = jnp.maximum(-lam, 0.0) + jnp.log1p(jnp.exp(-jnp.abs(lam)))
    log_a = (-LRU_C * r) * softplus_neg
    a = jnp.exp(log_a)
    z = -jnp.tanh(log_a) * (a * a + 1.0)
    root = jnp.where(z > 0.0, z * lax.rsqrt(z), 0.0)
    u = root * (gi * xc)

    ng = ts // SUBLANES
    sub = lax.broadcasted_iota(jnp.int32, (ts, 1), 0) % SUBLANES
    d = 1
    while d < SUBLANES:
        live = sub >= d
        u = jnp.where(live, a * pltpu.roll(u, d, 0) + u, u)
        a = jnp.where(live, a * pltpu.roll(a, d, 0), a)
        d *= 2
    nplane = rw // LANES
    last_rows = pl.ds(SUBLANES - 1, ng, stride=SUBLANES)
    for c in range(nplane):
        a_ref[c] = a[:, c * LANES:(c + 1) * LANES]
        u_ref[c] = u[:, c * LANES:(c + 1) * LANES]
    ag = jnp.concatenate([a_ref[c, last_rows, :] for c in range(nplane)], axis=1)
    ug = jnp.concatenate([u_ref[c, last_rows, :] for c in range(nplane)], axis=1)
    grow = lax.broadcasted_iota(jnp.int32, (ng, 1), 0)
    d = 1
    while d < ng:
        live = grow >= d
        ug = jnp.where(live, ag * pltpu.roll(ug, d, 0) + ug, ug)
        ag = jnp.where(live, ag * pltpu.roll(ag, d, 0), ag)
        d *= 2
    h0 = h_ref[...]
    hg = ag * h0 + ug
    hin_ref[...] = jnp.where(grow >= 1, pltpu.roll(hg, 1, 0), h0)
    h_ref[...] = hg[ng - 1:, :]
    for g2 in range(0, ng, 2):
        hs = []
        for g in (g2, g2 + 1):
            rows = slice(g * SUBLANES, (g + 1) * SUBLANES)
            a_g = jnp.concatenate([a_ref[c, rows, :] for c in range(nplane)], axis=1)
            u_g = jnp.concatenate([u_ref[c, rows, :] for c in range(nplane)], axis=1)
            hs.append(a_g * hin_ref[g:g + 1, :] + u_g)
        rows = slice(g2 * SUBLANES, (g2 + 2) * SUBLANES)
        gated = jnp.concatenate(hs, axis=0) * jax.nn.gelu(rec_ref[0, rows, rw:])
        o_ref[0, rows, :] = gated.astype(o_ref.dtype)


def _rglru(rec, conv_w, conv_b, w_gates_bf, b_a, b_i, lam):
    b, s, two_rw = rec.shape
    rw = two_rw // 2
    ts = min(ROW_TILE, s)
    full = lambda shape: pl.BlockSpec(shape, lambda b_, t: (0,) * len(shape))
    return pl.pallas_call(
        functools.partial(_rglru_kernel, rw=rw),
        grid=(b, s // ts),
        in_specs=[pl.BlockSpec((1, ts, two_rw), lambda b_, t: (b_, t, 0)),
                  full((CONV_WIDTH, rw)), full((1, rw)), full((rw, two_rw)),
                  full((1, rw)), full((1, rw)), full((1, rw))],
        out_specs=pl.BlockSpec((1, ts, rw), lambda b_, t: (b_, t, 0)),
        out_shape=jax.ShapeDtypeStruct((b, s, rw), BF16),
        scratch_shapes=[pltpu.VMEM((SUBLANES, rw), F32), pltpu.VMEM((1, rw), F32),
                        pltpu.VMEM((rw // LANES, ts, LANES), F32), pltpu.VMEM((rw // LANES, ts, LANES), F32),
                        pltpu.VMEM((ts // SUBLANES, rw), F32), pltpu.VMEM((ts + SUBLANES, rw), F32)],
        compiler_params=_cparams(("parallel", "arbitrary")),
        name="rglru",
    )(rec, conv_w, conv_b, w_gates_bf, b_a, b_i, lam)


def _block_diag(w):
    n, c, d = w.shape
    eye = jnp.eye(n, dtype=w.dtype)
    return (w[:, :, None, :] * eye[:, None, :, None]).reshape(n * c, n * d)


def _layer_norm(z, g, b):
    mu = jnp.mean(z, axis=-1, keepdims=True)
    zc = z - mu
    var = jnp.mean(zc * zc, axis=-1, keepdims=True)
    return zc * lax.rsqrt(var + LN_EPS) * g + b


def _rms(z, g):
    return z * lax.rsqrt(jnp.mean(z * z, axis=-1, keepdims=True) + RMS_EPS) * g


def _route(logits):
    lane = lax.broadcasted_iota(jnp.int32, logits.shape, 1).astype(F32)
    neg_inf = -jnp.inf
    first_lane = lambda mask: jnp.min(jnp.where(mask, lane, float(LANES)), axis=-1, keepdims=True)
    gl = jnp.where(lane < N_GROUPS, logits, neg_inf)
    gmax = jnp.max(gl, axis=-1, keepdims=True)
    gsum = jnp.sum(jnp.exp(gl - gmax), axis=-1, keepdims=True)
    g_p = 1.0 / gsum
    g_idx = first_lane(gl == gmax)
    lo = N_GROUPS + EXPERTS_PER_GROUP * g_idx
    emask = (lane >= lo) & (lane < lo + EXPERTS_PER_GROUP)
    el = jnp.where(emask, logits, neg_inf)
    emax = jnp.max(el, axis=-1, keepdims=True)
    ee = jnp.exp(el - emax)
    ep = ee / jnp.sum(ee, axis=-1, keepdims=True)
    e1 = jnp.max(ep, axis=-1, keepdims=True)
    i1 = first_lane(emask & (ep == e1))
    rest = emask & (lane != i1)
    ep2 = jnp.where(rest, ep, neg_inf)
    e2 = jnp.max(ep2, axis=-1, keepdims=True)
    i2 = first_lane(rest & (ep2 == e2))
    denom = e1 + e2
    w1 = g_p * e1 / denom
    w2 = g_p * e2 / denom
    comb = jnp.where(lane == i1, w1, 0.0) + jnp.where(lane == i2, w2, 0.0)
    return jnp.where(lane == 0, g_idx, comb)


def _outproj_kernel(attn_ref, rec_ref, x_ref, ga_ref, gr_ref, w_ref, g_ref, b_ref, wr_ref, br_ref,
                    x1_ref, gidx_ref, *, alpha, d, n_sub):
    sub = x_ref.shape[0] // n_sub
    rows = [slice(k * sub, (k + 1) * sub) for k in range(n_sub)]
    mixed = [jnp.concatenate([_rms(attn_ref[r, :].astype(F32), ga_ref[...]),
                              _rms(rec_ref[r, :].astype(F32), gr_ref[...])],
                             axis=-1).astype(BF16) for r in rows]
    w = w_ref[...].astype(BF16)
    h = [jnp.dot(m, w, preferred_element_type=F32) for m in mixed]
    logits = []
    for r, hk in zip(rows, h):
        x1 = _layer_norm(alpha * x_ref[r, :] + hk, g_ref[...], b_ref[...])
        x1_ref[r, 0:d] = x1
        logits.append(jnp.dot(x1.astype(BF16), wr_ref[...], preferred_element_type=F32) + br_ref[...])
    pick = ((lax.broadcasted_iota(jnp.int32, (SUBLANES, LANES), 0) == 0)
            & (lax.broadcasted_iota(jnp.int32, (SUBLANES, LANES), 1) == 0)).astype(BF16)
    for r, lg in zip(rows, logits):
        route = _route(lg)
        x1_ref[r, d:] = route
        moved = lax.dot_general(pick, route.astype(BF16), (((1,), (1,)), ((), ())), preferred_element_type=F32)
        gidx_ref[0:1, r] = moved[0:1, :]


def _out_proj(attn2, rec2, x2, g_a, g_r, w_out_all, layer, ln_g, ln_b, w_router_bf, b_router, alpha):
    t, d = x2.shape
    aw, rw = attn2.shape[1], rec2.shape[1]
    tm = min(OUT_TILE, t)
    n_sub = 2 if tm == OUT_TILE else 1
    row = lambda w: pl.BlockSpec((tm, w), lambda i: (i, 0))
    full = lambda shape: pl.BlockSpec(shape, lambda i: (0,) * len(shape))
    return pl.pallas_call(
        functools.partial(_outproj_kernel, alpha=alpha, d=d, n_sub=n_sub),
        grid=(t // tm,),
        in_specs=[row(aw), row(rw), row(d), full((1, aw)), full((1, rw)),
                  pl.BlockSpec((None, aw + rw, d), lambda i: (layer, 0, 0)),
                  full((1, d)), full((1, d)), full((d, LANES)), full((1, LANES))],
        out_specs=[row(d + LANES), pl.BlockSpec((1, tm), lambda i: (0, i))],
        out_shape=[jax.ShapeDtypeStruct((t, d + LANES), F32), jax.ShapeDtypeStruct((1, t), F32)],
        compiler_params=_cparams(("parallel",)),
        name="out_proj_ln_router",
    )(attn2, rec2, x2, g_a, g_r, w_out_all, ln_g, ln_b, w_router_bf, b_router)


def _dispatch_plan(gidx, tm):
    t = gidx.shape[0]
    assert t % tm == 0
    n_tiles = t // tm + N_GROUPS - 1
    counts = jnp.sum((gidx[:, None] == jnp.arange(N_GROUPS)[None, :]).astype(jnp.int32), axis=0)
    tiles_g = (counts + tm - 1) // tm
    tile_end = jnp.cumsum(tiles_g)
    tile_start = tile_end - tiles_g
    n_used = tile_end[-1]
    sorted_start = jnp.cumsum(counts) - counts
    order = jnp.argsort(gidx, stable=True).astype(jnp.int32)
    tile = jnp.arange(n_tiles, dtype=jnp.int32)
    tg = jnp.sum((tile_end[None, :] <= tile[:, None]).astype(jnp.int32), axis=1)
    used = tile < n_used
    tg = jnp.where(used, tg, tg[n_used - 1])
    row_off = (tile - tile_start[tg]) * tm
    nvalid = jnp.where(used, jnp.clip(counts[tg] - row_off, 0, tm), 0).astype(jnp.int32)
    r = jnp.arange(tm, dtype=jnp.int32)[None, :]
    pos = sorted_start[tg][:, None] + row_off[:, None] + jnp.minimum(r, nvalid[:, None] - 1)
    src = order[jnp.clip(pos, 0, t - 1)]
    return src.reshape(-1), tg.astype(jnp.int32), nvalid


def _moe_kernel(src_ref, tg_ref, nv_ref, x_hbm, wg_ref, wu_ref, wd_ref, g_ref, b_ref, out_hbm,
                xbuf0, xbuf1, obuf0, obuf1, z_ref, gsem, ssem, *, alpha, tm, d, n_tiles):
    i = pl.program_id(0)
    nxt = jnp.minimum(i + 1, n_tiles - 1)
    prv = jnp.maximum(i - 1, 0)
    nv_prv = jnp.where(i > 0, nv_ref[prv], 0)
    pp = jnp.maximum(i - 2, 0)
    nv_pp = jnp.where(i > 1, nv_ref[pp], 0)
    prv_full = nv_prv == tm

    def gather_row(tile, r, xbuf, sem):
        tok = src_ref[tile * tm + r]
        return pltpu.make_async_copy(x_hbm.at[pl.ds(tok, 1), :], xbuf.at[pl.ds(r, 1), :], sem)

    def scatter_row(tile, r, obuf, sem):
        tok = src_ref[tile * tm + r]
        return pltpu.make_async_copy(obuf.at[pl.ds(r, 1), :], out_hbm.at[pl.ds(tok, 1), :], sem)

    gather_tile = lambda xbuf, sem: pltpu.make_async_copy(x_hbm.at[pl.ds(0, tm), :], xbuf.at[pl.ds(0, tm), :], sem)
    scatter_tile = lambda obuf, sem: pltpu.make_async_copy(obuf.at[pl.ds(0, tm), :], out_hbm.at[pl.ds(0, tm), :], sem)

    def for_rows(n, fn, unroll):
        def body(r, _):
            fn(r)
            return 0
        lax.fori_loop(0, n, body, 0, unroll=unroll)

    def step(x_cur, x_nxt, o_cur, o_prv, g_cur, g_nxt, s_cur, s_prv):
        @pl.when(i == 0)
        def _prologue():
            for_rows(tm, lambda r: gather_row(0, r, x_cur, g_cur).start(), 8)
            x_cur[tm:, :] = jnp.zeros((SUBLANES, d + LANES), F32)
            x_nxt[tm:, :] = jnp.zeros((SUBLANES, d + LANES), F32)
            o_cur[tm:, :] = jnp.zeros((SUBLANES, d), F32)
            o_prv[tm:, :] = jnp.zeros((SUBLANES, d), F32)

        def experts(inline_scatter):
            gather_tile(x_cur, g_cur).wait()
            x = x_cur[0:tm, 0:d]
            route = x_cur[0:tm, d:]
            xb = x.astype(BF16)
            lane = lax.broadcasted_iota(jnp.int32, route.shape, 1)
            base = N_GROUPS + EXPERTS_PER_GROUP * tg_ref[i]
            rows_per = tm // EXPERTS_PER_GROUP
            de = wg_ref.shape[2]
            def gate_up(e):
                for r in range(e * rows_per, (e + 1) * rows_per):
                    gather_row(nxt, r, x_nxt, g_nxt).start()
                zero_row = x_nxt[tm:tm + 1, 0:de]
                if inline_scatter:
                    for r in range(e * rows_per // 2, (e + 1) * rows_per // 2):
                        scatter_row(prv, r, o_prv, s_prv).start()
                    zero_row = zero_row + o_prv[tm:tm + 1, 0:de]
                gate = jnp.dot(xb, wg_ref[e], preferred_element_type=F32)
                up = jnp.dot(xb, wu_ref[e], preferred_element_type=F32)
                return gate, up, zero_row

            def activate(gate, up, zero_row):
                return (jax.nn.silu(gate) * up + zero_row).astype(BF16)

            def down(e, hidden):
                w_e = jnp.sum(jnp.where(lane == base + e, route, 0.0), axis=-1, keepdims=True)
                return w_e * jnp.dot(hidden, wd_ref[e], preferred_element_type=F32)

            n_exp = EXPERTS_PER_GROUP
            pre = {0: gate_up(0), 1: gate_up(1)}
            hid = {}
            y = None
            for t in range(n_exp + 1):
                if t + 2 < n_exp:
                    pre[t + 2] = gate_up(t + 2)
                if t < n_exp:
                    hid[t] = activate(*pre.pop(t))
                if t >= 1:
                    contrib = down(t - 1, hid.pop(t - 1))
                    y = contrib if y is None else y + contrib
            z_ref[...] = alpha * x + y


        @pl.when(prv_full)
        def _experts_after_full_tile():
            experts(True)

        @pl.when(jnp.logical_not(prv_full))
        def _experts_after_partial_tile():
            experts(False)


        @pl.when(nv_pp == tm)
        def _drain_full():
            scatter_tile(o_cur, s_cur).wait()

        @pl.when(nv_pp < tm)
        def _drain_partial():
            for_rows(nv_pp, lambda r: scatter_row(pp, r, o_cur, s_cur).wait(), 1)

        @pl.when(prv_full)
        def _finish_after_full_tile():
            o_cur[0:tm, :] = _layer_norm(z_ref[...], g_ref[...], b_ref[...])
            for r in range(tm // 2, tm):
                scatter_row(prv, r, o_prv, s_prv).start()

        @pl.when(jnp.logical_not(prv_full))
        def _finish_after_partial_tile():
            o_cur[0:tm, :] = _layer_norm(z_ref[...], g_ref[...], b_ref[...])
            for_rows(nv_prv, lambda r: scatter_row(prv, r, o_prv, s_prv).start(), 1)

        @pl.when(i == n_tiles - 1)
        def _epilogue():
            gather_tile(x_nxt, g_nxt).wait()
            for_rows(nv_prv, lambda r: scatter_row(prv, r, o_prv, s_prv).wait(), 1)
            for_rows(nv_ref[i], lambda r: scatter_row(i, r, o_cur, s_cur).start(), 1)
            for_rows(nv_ref[i], lambda r: scatter_row(i, r, o_cur, s_cur).wait(), 1)

    @pl.when(i % 2 == 0)
    def _even():
        step(xbuf0, xbuf1, obuf0, obuf1, gsem.at[0], gsem.at[1], ssem.at[0], ssem.at[1])

    @pl.when(i % 2 == 1)
    def _odd():
        step(xbuf1, xbuf0, obuf1, obuf0, gsem.at[1], gsem.at[0], ssem.at[1], ssem.at[0])


def _moe(x1e, gidx, t, wg_bf, wu_bf, wd_bf, layer, ln_g, ln_b, alpha, d):
    de = wg_bf.shape[3]
    tm = min(ROW_TILE, t)
    src, tg, nvalid = _dispatch_plan(gidx, tm)
    n_tiles = tg.shape[0]
    epg = EXPERTS_PER_GROUP
    full = lambda shape: pl.BlockSpec(shape, lambda i, *_: (0,) * len(shape))
    grid_spec = pltpu.PrefetchScalarGridSpec(
        num_scalar_prefetch=3,
        grid=(n_tiles,),
        in_specs=[pl.BlockSpec(memory_space=pl.ANY),
                  pl.BlockSpec((None, epg, d, de), lambda i, src_, tg_, nv_: (layer, tg_[i], 0, 0)),
                  pl.BlockSpec((None, epg, d, de), lambda i, src_, tg_, nv_: (layer, tg_[i], 0, 0)),
                  pl.BlockSpec((None, epg, de, d), lambda i, src_, tg_, nv_: (layer, tg_[i], 0, 0)),
                  full((1, d)), full((1, d))],
        out_specs=pl.BlockSpec(memory_space=pl.ANY),
        scratch_shapes=[pltpu.VMEM((tm + SUBLANES, d + LANES), F32), pltpu.VMEM((tm + SUBLANES, d + LANES), F32),
                        pltpu.VMEM((tm + SUBLANES, d), F32), pltpu.VMEM((tm + SUBLANES, d), F32),
                        pltpu.VMEM((tm, d), F32),
                        pltpu.SemaphoreType.DMA((2,)),
                        pltpu.SemaphoreType.DMA((2,))],
    )
    return pl.pallas_call(
        functools.partial(_moe_kernel, alpha=alpha, tm=tm, d=d, n_tiles=n_tiles),
        grid_spec=grid_spec,
        out_shape=jax.ShapeDtypeStruct((t, d), F32),
        compiler_params=_cparams(("arbitrary",)),
        name="moe_ln",
    )(src, tg, nvalid, x1e, wg_bf, wu_bf, wd_bf, ln_g, ln_b)


def kernel(x, w_in, conv_w, conv_b, w_rg_a, b_rg_a, w_rg_i, b_rg_i, lru_lambda, g_attn_norm, g_rec_norm, w_out,
           ln1_g, ln1_b, w_router_group, b_router_group, w_router_expert, b_router_expert, w_gate, w_up, w_down,
           ln2_g, ln2_b):
    b, s, d = x.shape
    depth = w_in.shape[0]
    rw = conv_w.shape[2]
    aw = (w_in.shape[2] - 2 * rw) // 3
    alpha = (2 * depth) ** 0.25
    tables = _rope_tables(s)
    row = lambda v: v.reshape(1, -1)

    wg_bf, wu_bf, wd_bf = w_gate.astype(BF16), w_up.astype(BF16), w_down.astype(BF16)
    x2 = x.reshape(b * s, d)
    for l in range(depth):
        q, k, v, rec = _in_proj(x2, w_in, l, tables, s, aw)
        attn = _moba_attention(q.reshape(b, s, aw), k.reshape(b, s, aw), v.reshape(b, s, aw))
        w_gates = jnp.concatenate([_block_diag(w_rg_a[l]), _block_diag(w_rg_i[l])], axis=1).astype(BF16)
        rec_out = _rglru(rec.reshape(b, s, 2 * rw), conv_w[l], row(conv_b[l]), w_gates,
                         row(b_rg_a[l]), row(b_rg_i[l]), row(lru_lambda[l]))
        n_route = N_GROUPS + N_GROUPS * EXPERTS_PER_GROUP
        w_router = jnp.concatenate([w_router_group[l], w_router_expert[l]], axis=1)
        w_router = jnp.pad(w_router, ((0, 0), (0, LANES - n_route))).astype(BF16)
        b_router = jnp.pad(jnp.concatenate([b_router_group[l], b_router_expert[l]]), (0, LANES - n_route))
        x1e, gidx = _out_proj(attn.reshape(b * s, aw), rec_out.reshape(b * s, rw), x2,
                        row(g_attn_norm[l]), row(g_rec_norm[l]), w_out, l,
                        row(ln1_g[l]), row(ln1_b[l]), w_router, row(b_router), alpha)
        x2 = _moe(x1e, gidx[0].astype(jnp.int32), b * s, wg_bf, wu_bf, wd_bf, l,
                  row(ln2_g[l]), row(ln2_b[l]), alpha, d)
    return x2.reshape(b, s, d)
```

```python
import functools

import jax
import jax.numpy as jnp
from jax import lax
from jax.experimental import pallas as pl
from jax.experimental.pallas import tpu as pltpu

F32 = jnp.float32
BF16 = jnp.bfloat16

HEAD_DIM = 64
ROT_DIM = HEAD_DIM // 4
ROPE_THETA = 500000.0
MOBA_BLOCK = 256
MOBA_TOPK = 3
CONV_WIDTH = 4
LRU_C = 8.0
N_GROUPS = 4
EXPERTS_PER_GROUP = 4
LN_EPS = 1e-5
RMS_EPS = 1e-6

LANES = 128
SUBLANES = 8
NEG_BIG = -1e30
LOG2E = 1.4426950408889634
Q_SCALE = HEAD_DIM ** -0.5 * LOG2E
AHEAD, BEHIND = 3, 1
VT_ROWS = HEAD_DIM + 16
VMEM_LIMIT = 56 * 1024 * 1024
ROW_TILE = 512
OUT_TILE = 1024


def _cparams(sem):
    return pltpu.CompilerParams(dimension_semantics=sem, vmem_limit_bytes=VMEM_LIMIT)


def _inproj_kernel(x_ref, w_ref, c_ref, s1_ref, s2_ref, q_ref, k_ref, v_ref, rec_ref, *, aw):
    xb = x_ref[...].astype(BF16)
    cos, sin_lo, sin_hi = c_ref[...], s1_ref[...], s2_ref[...]

    def rope(t):
        outs = []
        for c in range(aw // LANES):
            tc = t[:, c * LANES:(c + 1) * LANES]
            outs.append(tc * cos + pltpu.roll(tc, 8, 1) * sin_lo + pltpu.roll(tc, LANES - 8, 1) * sin_hi)
        return jnp.concatenate(outs, axis=1)

    q = jnp.dot(xb, w_ref[:, 0:aw].astype(BF16), preferred_element_type=F32)
    q_ref[...] = (rope(q) * Q_SCALE).astype(BF16)
    k = jnp.dot(xb, w_ref[:, aw:2 * aw].astype(BF16), preferred_element_type=F32)
    k_ref[...] = rope(k).astype(BF16)
    v = jnp.dot(xb, w_ref[:, 2 * aw:3 * aw].astype(BF16), preferred_element_type=F32)
    v_ref[...] = v.astype(BF16)
    rec_ref[...] = jnp.dot(xb, w_ref[:, 3 * aw:].astype(BF16), preferred_element_type=F32)


def _rope_tables(seq):
    half = ROT_DIM // 2
    inv_freq = ROPE_THETA ** (-jnp.arange(half, dtype=F32) * 2.0 / ROT_DIM)
    ang = jnp.arange(seq).astype(F32)[:, None] * inv_freq[None, :]
    cos, sin = jnp.cos(ang), jnp.sin(ang)
    zeros = jnp.zeros((seq, HEAD_DIM - ROT_DIM), F32)
    zh = jnp.zeros((seq, half), F32)
    c64 = jnp.concatenate([cos, cos, zeros + 1.0], axis=1)
    lo64 = jnp.concatenate([zh, sin, zeros], axis=1)
    hi64 = jnp.concatenate([-sin, zh, zeros], axis=1)
    rep = LANES // HEAD_DIM
    return jnp.tile(c64, (1, rep)), jnp.tile(lo64, (1, rep)), jnp.tile(hi64, (1, rep))


def _in_proj(x2, w_all, layer, tables, seq, aw):
    t, d = x2.shape
    n = w_all.shape[2]
    tm = min(ROW_TILE, seq)
    nseq = seq // tm
    tab_spec = pl.BlockSpec((tm, LANES), lambda i: (i % nseq, 0))
    return pl.pallas_call(
        functools.partial(_inproj_kernel, aw=aw),
        grid=(t // tm,),
        in_specs=[pl.BlockSpec((tm, d), lambda i: (i, 0)),
                  pl.BlockSpec((None, d, n), lambda i: (layer, 0, 0)),
                  tab_spec, tab_spec, tab_spec],
        out_specs=[pl.BlockSpec((tm, aw), lambda i: (i, 0)),
                   pl.BlockSpec((tm, aw), lambda i: (i, 0)),
                   pl.BlockSpec((tm, aw), lambda i: (i, 0)),
                   pl.BlockSpec((tm, n - 3 * aw), lambda i: (i, 0))],
        out_shape=[jax.ShapeDtypeStruct((t, aw), BF16),
                   jax.ShapeDtypeStruct((t, aw), BF16),
                   jax.ShapeDtypeStruct((t, aw), BF16),
                   jax.ShapeDtypeStruct((t, n - 3 * aw), F32)],
        compiler_params=_cparams(("parallel",)),
        name="in_proj",
    )(x2, w_all, *tables)


def _attn_kernel(q_ref, k_ref, v_ref, o_ref, vt_ref, km_ref, bias_ref, m_ref, acc_ref, *, nb, npair):
    blk = MOBA_BLOCK
    i = pl.program_id(1)
    lane = lax.broadcasted_iota(jnp.int32, (1, LANES), 1)
    head_lanes = (lane < HEAD_DIM, lane >= HEAD_DIM)
    nt = (((1,), (1,)), ((), ()))
    heads = [(p, h) for p in range(npair) for h in range(2)]
    pair_lanes = lambda p: slice(p * LANES, (p + 1) * LANES)

    @pl.when(i == 0)
    def _prepare():
        def prep(j, _):
            off = pl.multiple_of(j * blk, blk)
            for p in range(npair):
                kb = k_ref[0, pl.ds(off, blk), pair_lanes(p)].astype(F32)
                km = jnp.sum(kb, axis=0, keepdims=True) * (1.0 / blk)
                km_ref[2 * p, pl.ds(j, 1), :] = jnp.where(head_lanes[0], km, 0.0)
                km_ref[2 * p + 1, pl.ds(j, 1), :] = jnp.where(head_lanes[1], km, 0.0)
                vt = v_ref[0, pl.ds(off, blk), pair_lanes(p)].astype(F32).T
                vt_ref[2 * p, 0:HEAD_DIM, pl.ds(off, blk)] = vt[0:HEAD_DIM].astype(BF16)
                vt_ref[2 * p + 1, 0:HEAD_DIM, pl.ds(off, blk)] = vt[HEAD_DIM:].astype(BF16)
            return 0

        lax.fori_loop(0, nb, prep, 0)
        vt_ref[:, HEAD_DIM:, :] = jnp.ones((2 * npair, VT_ROWS - HEAD_DIM, nb * blk), BF16)

    qh = []
    for p, h in heads:
        q_pair = q_ref[0, :, pair_lanes(p)]
        qh.append(jnp.where(head_lanes[h], q_pair, jnp.zeros_like(q_pair)))

    def block_gate(n):
        p, h = heads[n]
        n_iota = lax.broadcasted_iota(jnp.int32, (nb, blk), 0)
        valid = n_iota < i
        g = lax.dot_general(km_ref[n], q_ref[0, :, pair_lanes(p)].astype(F32), nt, preferred_element_type=F32)
        g = jnp.where(valid, g, -jnp.inf)
        bias = jnp.full((nb, blk), NEG_BIG, F32)
        for _ in range(min(MOBA_TOPK, nb)):
            top = jnp.max(g, axis=0, keepdims=True)
            first = jnp.min(jnp.where(g == top, n_iota, nb), axis=0, keepdims=True)
            hit = n_iota == first
            bias = jnp.where(hit & valid, 0.0, bias)
            g = jnp.where(hit, -jnp.inf, g)
        bias_ref[n] = bias

    off_i = pl.multiple_of(i * blk, blk)
    kpos = lax.broadcasted_iota(jnp.int32, (blk, blk), 0)
    qpos = lax.broadcasted_iota(jnp.int32, (blk, blk), 1)
    causal = kpos <= qpos

    def scores(n, off, rows):
        kb = k_ref[0, pl.ds(off, rows), pair_lanes(heads[n][0])]
        return lax.dot_general(kb, qh[n], nt, preferred_element_type=F32)

    def interleave(stage_scores, stage_probs, stage_values, count):
        s_q = {n: stage_scores(n) for n in range(min(AHEAD, count))}
        p_q = {}
        for t in range(count + BEHIND):
            if t + AHEAD < count:
                s_q[t + AHEAD] = stage_scores(t + AHEAD)
            if t < count:
                p_q[t] = stage_probs(t, s_q.pop(t))
            if t >= BEHIND:
                stage_values(t - BEHIND, *p_q.pop(t - BEHIND))

    def own_probs(n, s):
        s = jnp.where(causal, s, NEG_BIG)
        m0 = jnp.max(s, axis=0, keepdims=True)
        m_ref[n] = m0
        return (jnp.exp2(s - m0).astype(BF16),)

    def own_values(n, pr):
        acc_ref[n] = jnp.dot(vt_ref[n, :, pl.ds(off_i, blk)], pr, preferred_element_type=F32)
        block_gate(n)

    nh = len(heads)
    interleave(lambda n: scores(n, off_i, blk), own_probs, own_values, nh)

    def run_pairs(first_pair, n_pairs):
        def j0_of(t):
            return 2 * (first_pair + t // nh)

        def off_of(t):
            return pl.multiple_of(j0_of(t) * blk, 2 * blk)

        def probs(t, s):
            n, j0 = t % nh, j0_of(t)
            b0 = bias_ref[n, pl.ds(j0, 1), :]
            b1 = bias_ref[n, pl.ds(j0 + 1, 1), :]
            s0, s1 = s[0:blk], s[blk:]
            mj = jnp.maximum(jnp.max(s0, axis=0, keepdims=True) + b0, jnp.max(s1, axis=0, keepdims=True) + b1)
            m_run = m_ref[n]
            m_new = jnp.maximum(m_run, mj)
            m_ref[n] = m_new
            alpha = jnp.exp2(m_run - m_new)
            p0 = jnp.exp2(s0 - (m_new - b0)).astype(BF16)
            p1 = jnp.exp2(s1 - (m_new - b1)).astype(BF16)
            return alpha, p0, p1

        def accumulate(t, alpha, p0, p1):
            n, off = t % nh, off_of(t)
            pv = (jnp.dot(vt_ref[n, :, pl.ds(off, blk)], p0, preferred_element_type=F32)
                  + jnp.dot(vt_ref[n, :, pl.ds(off + blk, blk)], p1, preferred_element_type=F32))
            acc_ref[n] = alpha * acc_ref[n] + pv

        interleave(lambda t: scores(t % nh, off_of(t), 2 * blk), probs, accumulate, n_pairs * nh)

    n_pairs = (i + 1) // 2

    def quad_body(jj, _):
        run_pairs(4 * jj, 4)
        return 0

    lax.fori_loop(0, n_pairs // 4, quad_body, 0)

    @pl.when((n_pairs // 2) % 2 == 1)
    def _pair_of_pairs():
        run_pairs(4 * (n_pairs // 4), 2)

    @pl.when(n_pairs % 2 == 1)
    def _last_pair():
        run_pairs(n_pairs - 1, 1)

    for p in range(npair):
        outs = []
        for h in range(2):
            acc = acc_ref[2 * p + h]
            outs.append(acc[0:HEAD_DIM] / acc[HEAD_DIM:HEAD_DIM + 1])
        o_ref[0, :, pair_lanes(p)] = jnp.concatenate(outs, axis=0).T.astype(o_ref.dtype)


def _moba_attention(q, k, v):
    b, s, aw = q.shape
    nb = s // MOBA_BLOCK
    npair = aw // LANES
    nh = 2 * npair
    return pl.pallas_call(
        functools.partial(_attn_kernel, nb=nb, npair=npair),
        grid=(b, nb),
        in_specs=[pl.BlockSpec((1, MOBA_BLOCK, aw), lambda b_, i: (b_, i, 0)),
                  pl.BlockSpec((1, s, aw), lambda b_, i: (b_, 0, 0)),
                  pl.BlockSpec((1, s, aw), lambda b_, i: (b_, 0, 0))],
        out_specs=pl.BlockSpec((1, MOBA_BLOCK, aw), lambda b_, i: (b_, i, 0)),
        out_shape=jax.ShapeDtypeStruct((b, s, aw), BF16),
        scratch_shapes=[pltpu.VMEM((nh, VT_ROWS, s), BF16),
                        pltpu.VMEM((nh, nb, LANES), F32),
                        pltpu.VMEM((nh, nb, MOBA_BLOCK), F32),
                        pltpu.VMEM((nh, 1, MOBA_BLOCK), F32),
                        pltpu.VMEM((nh, VT_ROWS, MOBA_BLOCK), F32)],
        compiler_params=_cparams(("parallel", "arbitrary")),
        name="moba_attn",
    )(q, k, v)


def _rglru_kernel(rec_ref, cw_ref, cb_ref, wg_ref, ba_ref, bi_ref, lam_ref, o_ref,
                  tail_ref, h_ref, a_ref, u_ref, hin_ref, xs_ref, *, rw):
    ts = rec_ref.shape[1]
    t_idx = pl.program_id(1)

    @pl.when(t_idx == 0)
    def _reset():
        tail_ref[...] = jnp.zeros_like(tail_ref)
        h_ref[...] = jnp.zeros_like(h_ref)

    xr = rec_ref[0, :, 0:rw]
    xs_ref[0:SUBLANES, :] = tail_ref[...]
    xs_ref[SUBLANES:, :] = xr
    tail_ref[...] = xr[ts - SUBLANES:, :]
    xc = xr * cw_ref[CONV_WIDTH - 1:CONV_WIDTH, :] + cb_ref[...]
    for d in range(1, CONV_WIDTH):
        xc = xc + xs_ref[SUBLANES - d:SUBLANES - d + ts, :] * cw_ref[CONV_WIDTH - 1 - d:CONV_WIDTH - d, :]

    pre = jnp.dot(xc.astype(BF16), wg_ref[...], preferred_element_type=F32)
    sigmoid = lambda v: 0.5 * jnp.tanh(0.5 * v) + 0.5
    r = sigmoid(pre[:, 0:rw] + ba_ref[...])
    gi = sigmoid(pre[:, rw:] + bi_ref[...])
    lam = lam_ref[...]
    softplus_neg = jnp.maximum(-lam, 0.0) + jnp.log1p(jnp.exp(-jnp.abs(lam)))
    log_a = (-LRU_C * r) * softplus_neg
    a = jnp.exp(log_a)
    z = -jnp.tanh(log_a) * (a * a + 1.0)
    root = jnp.where(z > 0.0, z * lax.rsqrt(z), 0.0)
    u = root * (gi * xc)

    ng = ts // SUBLANES
    sub = lax.broadcasted_iota(jnp.int32, (ts, 1), 0) % SUBLANES
    d = 1
    while d < SUBLANES:
        live = sub >= d
        u = jnp.where(live, a * pltpu.roll(u, d, 0) + u, u)
        a = jnp.where(live, a * pltpu.roll(a, d, 0), a)
        d *= 2
    nplane = rw // LANES
    last_rows = pl.ds(SUBLANES - 1, ng, stride=SUBLANES)
    for c in range(nplane):
        a_ref[c] = a[:, c * LANES:(c + 1) * LANES]
        u_ref[c] = u[:, c * LANES:(c + 1) * LANES]
    ag = jnp.concatenate([a_ref[c, last_rows, :] for c in range(nplane)], axis=1)
    ug = jnp.concatenate([u_ref[c, last_rows, :] for c in range(nplane)], axis=1)
    grow = lax.broadcasted_iota(jnp.int32, (ng, 1), 0)
    d = 1
    while d < ng:
        live = grow >= d
        ug = jnp.where(live, ag * pltpu.roll(ug, d, 0) + ug, ug)
        ag = jnp.where(live, ag * pltpu.roll(ag, d, 0), ag)
        d *= 2
    h0 = h_ref[...]
    hg = ag * h0 + ug
    hin_ref[...] = jnp.where(grow >= 1, pltpu.roll(hg, 1, 0), h0)
    h_ref[...] = hg[ng - 1:, :]
    for g2 in range(0, ng, 2):
        hs = []
        for g in (g2, g2 + 1):
            rows = slice(g * SUBLANES, (g + 1) * SUBLANES)
            a_g = jnp.concatenate([a_ref[c, rows, :] for c in range(nplane)], axis=1)
            u_g = jnp.concatenate([u_ref[c, rows, :] for c in range(nplane)], axis=1)
            hs.append(a_g * hin_ref[g:g + 1, :] + u_g)
        rows = slice(g2 * SUBLANES, (g2 + 2) * SUBLANES)
        gated = jnp.concatenate(hs, axis=0) * jax.nn.gelu(rec_ref[0, rows, rw:])
        o_ref[0, rows, :] = gated.astype(o_ref.dtype)


def _rglru(rec, conv_w, conv_b, w_gates_bf, b_a, b_i, lam):
    b, s, two_rw = rec.shape
    rw = two_rw // 2
    ts = min(ROW_TILE, s)
    full = lambda shape: pl.BlockSpec(shape, lambda b_, t: (0,) * len(shape))
    return pl.pallas_call(
        functools.partial(_rglru_kernel, rw=rw),
        grid=(b, s // ts),
        in_specs=[pl.BlockSpec((1, ts, two_rw), lambda b_, t: (b_, t, 0)),
                  full((CONV_WIDTH, rw)), full((1, rw)), full((rw, two_rw)),
                  full((1, rw)), full((1, rw)), full((1, rw))],
        out_specs=pl.BlockSpec((1, ts, rw), lambda b_, t: (b_, t, 0)),
        out_shape=jax.ShapeDtypeStruct((b, s, rw), BF16),
        scratch_shapes=[pltpu.VMEM((SUBLANES, rw), F32), pltpu.VMEM((1, rw), F32),
                        pltpu.VMEM((rw // LANES, ts, LANES), F32), pltpu.VMEM((rw // LANES, ts, LANES), F32),
                        pltpu.VMEM((ts // SUBLANES, rw), F32), pltpu.VMEM((ts + SUBLANES, rw), F32)],
        compiler_params=_cparams(("parallel", "arbitrary")),
        name="rglru",
    )(rec, conv_w, conv_b, w_gates_bf, b_a, b_i, lam)


def _block_diag(w):
    n, c, d = w.shape
    eye = jnp.eye(n, dtype=w.dtype)
    return (w[:, :, None, :] * eye[:, None, :, None]).reshape(n * c, n * d)


def _layer_norm(z, g, b):
    mu = jnp.mean(z, axis=-1, keepdims=True)
    zc = z - mu
    var = jnp.mean(zc * zc, axis=-1, keepdims=True)
    return zc * lax.rsqrt(var + LN_EPS) * g + b


def _rms(z, g):
    return z * lax.rsqrt(jnp.mean(z * z, axis=-1, keepdims=True) + RMS_EPS) * g


def _route(logits):
    lane = lax.broadcasted_iota(jnp.int32, logits.shape, 1).astype(F32)
    neg_inf = -jnp.inf
    first_lane = lambda mask: jnp.min(jnp.where(mask, lane, float(LANES)), axis=-1, keepdims=True)
    gl = jnp.where(lane < N_GROUPS, logits, neg_inf)
    gmax = jnp.max(gl, axis=-1, keepdims=True)
    gsum = jnp.sum(jnp.exp(gl - gmax), axis=-1, keepdims=True)
    g_p = 1.0 / gsum
    g_idx = first_lane(gl == gmax)
    lo = N_GROUPS + EXPERTS_PER_GROUP * g_idx
    emask = (lane >= lo) & (lane < lo + EXPERTS_PER_GROUP)
    el = jnp.where(emask, logits, neg_inf)
    emax = jnp.max(el, axis=-1, keepdims=True)
    ee = jnp.exp(el - emax)
    ep = ee / jnp.sum(ee, axis=-1, keepdims=True)
    e1 = jnp.max(ep, axis=-1, keepdims=True)
    i1 = first_lane(emask & (ep == e1))
    rest = emask & (lane != i1)
    ep2 = jnp.where(rest, ep, neg_inf)
    e2 = jnp.max(ep2, axis=-1, keepdims=True)
    i2 = first_lane(rest & (ep2 == e2))
    denom = e1 + e2
    w1 = g_p * e1 / denom
    w2 = g_p * e2 / denom
    comb = jnp.where(lane == i1, w1, 0.0) + jnp.where(lane == i2, w2, 0.0)
    return jnp.where(lane == 0, g_idx, comb)


def _outproj_kernel(attn_ref, rec_ref, x_ref, ga_ref, gr_ref, w_ref, g_ref, b_ref, wr_ref, br_ref,
                    x1_ref, gidx_ref, *, alpha, d, n_sub):
    sub = x_ref.shape[0] // n_sub
    rows = [slice(k * sub, (k + 1) * sub) for k in range(n_sub)]
    mixed = [jnp.concatenate([_rms(attn_ref[r, :].astype(F32), ga_ref[...]),
                              _rms(rec_ref[r, :].astype(F32), gr_ref[...])],
                             axis=-1).astype(BF16) for r in rows]
    w = w_ref[...].astype(BF16)
    h = [jnp.dot(m, w, preferred_element_type=F32) for m in mixed]
    logits = []
    for r, hk in zip(rows, h):
        x1 = _layer_norm(alpha * x_ref[r, :] + hk, g_ref[...], b_ref[...])
        x1_ref[r, 0:d] = x1
        logits.append(jnp.dot(x1.astype(BF16), wr_ref[...], preferred_element_type=F32) + br_ref[...])
    pick = ((lax.broadcasted_iota(jnp.int32, (SUBLANES, LANES), 0) == 0)
            & (lax.broadcasted_iota(jnp.int32, (SUBLANES, LANES), 1) == 0)).astype(BF16)
    for r, lg in zip(rows, logits):
        route = _route(lg)
        x1_ref[r, d:] = route
        moved = lax.dot_general(pick, route.astype(BF16), (((1,), (1,)), ((), ())), preferred_element_type=F32)
        gidx_ref[0:1, r] = moved[0:1, :]


def _out_proj(attn2, rec2, x2, g_a, g_r, w_out_all, layer, ln_g, ln_b, w_router_bf, b_router, alpha):
    t, d = x2.shape
    aw, rw = attn2.shape[1], rec2.shape[1]
    tm = min(OUT_TILE, t)
    n_sub = 2 if tm == OUT_TILE else 1
    row = lambda w: pl.BlockSpec((tm, w), lambda i: (i, 0))
    full = lambda shape: pl.BlockSpec(shape, lambda i: (0,) * len(shape))
    return pl.pallas_call(
        functools.partial(_outproj_kernel, alpha=alpha, d=d, n_sub=n_sub),
        grid=(t // tm,),
        in_specs=[row(aw), row(rw), row(d), full((1, aw)), full((1, rw)),
                  pl.BlockSpec((None, aw + rw, d), lambda i: (layer, 0, 0)),
                  full((1, d)), full((1, d)), full((d, LANES)), full((1, LANES))],
        out_specs=[row(d + LANES), pl.BlockSpec((1, tm), lambda i: (0, i))],
        out_shape=[jax.ShapeDtypeStruct((t, d + LANES), F32), jax.ShapeDtypeStruct((1, t), F32)],
        compiler_params=_cparams(("parallel",)),
        name="out_proj_ln_router",
    )(attn2, rec2, x2, g_a, g_r, w_out_all, ln_g, ln_b, w_router_bf, b_router)


def _dispatch_plan(gidx, tm):
    t = gidx.shape[0]
    assert t % tm == 0
    n_tiles = t // tm + N_GROUPS - 1
    counts = jnp.sum((gidx[:, None] == jnp.arange(N_GROUPS)[None, :]).astype(jnp.int32), axis=0)
    tiles_g = (counts + tm - 1) // tm
    tile_end = jnp.cumsum(tiles_g)
    tile_start = tile_end - tiles_g
    n_used = tile_end[-1]
    sorted_start = jnp.cumsum(counts) - counts
    order = jnp.argsort(gidx, stable=True).astype(jnp.int32)
    tile = jnp.arange(n_tiles, dtype=jnp.int32)
    tg = jnp.sum((tile_end[None, :] <= tile[:, None]).astype(jnp.int32), axis=1)
    used = tile < n_used
    tg = jnp.where(used, tg, tg[n_used - 1])
    row_off = (tile - tile_start[tg]) * tm
    nvalid = jnp.where(used, jnp.clip(counts[tg] - row_off, 0, tm), 0).astype(jnp.int32)
    r = jnp.arange(tm, dtype=jnp.int32)[None, :]
    pos = sorted_start[tg][:, None] + row_off[:, None] + jnp.minimum(r, nvalid[:, None] - 1)
    src = order[jnp.clip(pos, 0, t - 1)]
    return src.reshape(-1), tg.astype(jnp.int32), nvalid


def _moe_kernel(src_ref, tg_ref, nv_ref, x_hbm, wg_ref, wu_ref, wd_ref, g_ref, b_ref, out_hbm,
                xbuf0, xbuf1, obuf0, obuf1, gsem, ssem, *, alpha, tm, d, n_tiles):
    i = pl.program_id(0)
    nxt = jnp.minimum(i + 1, n_tiles - 1)
    prv = jnp.maximum(i - 1, 0)
    nv_prv = jnp.where(i > 0, nv_ref[prv], 0)
    pp = jnp.maximum(i - 2, 0)
    nv_pp = jnp.where(i > 1, nv_ref[pp], 0)
    prv_full = nv_prv == tm

    def gather_row(tile, r, xbuf, sem):
        tok = src_ref[tile * tm + r]
        return pltpu.make_async_copy(x_hbm.at[pl.ds(tok, 1), :], xbuf.at[pl.ds(r, 1), :], sem)

    def scatter_row(tile, r, obuf, sem):
        tok = src_ref[tile * tm + r]
        return pltpu.make_async_copy(obuf.at[pl.ds(r, 1), :], out_hbm.at[pl.ds(tok, 1), :], sem)

    gather_tile = lambda xbuf, sem: pltpu.make_async_copy(x_hbm.at[pl.ds(0, tm), :], xbuf.at[pl.ds(0, tm), :], sem)
    scatter_tile = lambda obuf, sem: pltpu.make_async_copy(obuf, out_hbm.at[pl.ds(0, tm), :], sem)

    def for_rows(n, fn, unroll):
        def body(r, _):
            fn(r)
            return 0
        lax.fori_loop(0, n, body, 0, unroll=unroll)

    def step(x_cur, x_nxt, o_cur, o_prv, g_cur, g_nxt, s_cur, s_prv):
        @pl.when(i == 0)
        def _prologue():
            for_rows(tm, lambda r: gather_row(0, r, x_cur, g_cur).start(), 8)
            x_cur[tm:, :] = jnp.zeros((SUBLANES, d + LANES), F32)
            x_nxt[tm:, :] = jnp.zeros((SUBLANES, d + LANES), F32)

        gather_tile(x_cur, g_cur).wait()
        x = x_cur[0:tm, 0:d]
        route = x_cur[0:tm, d:]
        xb = x.astype(BF16)
        lane = lax.broadcasted_iota(jnp.int32, route.shape, 1)
        base = N_GROUPS + EXPERTS_PER_GROUP * tg_ref[i]
        rows_per = tm // EXPERTS_PER_GROUP
        de = wg_ref.shape[2]
        def gate_up(e):
            for r in range(e * rows_per, (e + 1) * rows_per):
                gather_row(nxt, r, x_nxt, g_nxt).start(priority=r % 2)
            zero_row = x_nxt[tm:tm + 1, 0:de]
            gate = jnp.dot(xb, wg_ref[e], preferred_element_type=F32)
            up = jnp.dot(xb, wu_ref[e], preferred_element_type=F32)
            return gate, up, zero_row

        def activate(gate, up, zero_row):
            return (jax.nn.silu(gate) * up + zero_row).astype(BF16)

        def down(e, hidden):
            w_e = jnp.sum(jnp.where(lane == base + e, route, 0.0), axis=-1, keepdims=True)
            return w_e * jnp.dot(hidden, wd_ref[e], preferred_element_type=F32)

        n_exp = EXPERTS_PER_GROUP
        pre = {0: gate_up(0), 1: gate_up(1)}
        hid = {}
        y = None
        for t in range(n_exp + 1):
            if t + 2 < n_exp:
                pre[t + 2] = gate_up(t + 2)
            if t < n_exp:
                hid[t] = activate(*pre.pop(t))
            if t >= 1:
                contrib = down(t - 1, hid.pop(t - 1))
                y = contrib if y is None else y + contrib
        z = alpha * x + y

        @pl.when(nv_pp == tm)
        def _drain_full():
            scatter_tile(o_cur, s_cur).wait()

        @pl.when(nv_pp < tm)
        def _drain_partial():
            for_rows(nv_pp, lambda r: scatter_row(pp, r, o_cur, s_cur).wait(), 1)

        @pl.when(prv_full)
        def _finish_after_full_tile():
            o_cur[...] = _layer_norm(z, g_ref[...], b_ref[...])
            for r in range(tm):
                scatter_row(prv, r, o_prv, s_prv).start(priority=r % 2)

        @pl.when(jnp.logical_not(prv_full))
        def _finish_after_partial_tile():
            o_cur[...] = _layer_norm(z, g_ref[...], b_ref[...])
            for_rows(nv_prv, lambda r: scatter_row(prv, r, o_prv, s_prv).start(), 1)

        @pl.when(i == n_tiles - 1)
        def _epilogue():
            gather_tile(x_nxt, g_nxt).wait()
            for_rows(nv_prv, lambda r: scatter_row(prv, r, o_prv, s_prv).wait(), 1)
            for_rows(nv_ref[i], lambda r: scatter_row(i, r, o_cur, s_cur).start(), 1)
            for_rows(nv_ref[i], lambda r: scatter_row(i, r, o_cur, s_cur).wait(), 1)

    @pl.when(i % 2 == 0)
    def _even():
        step(xbuf0, xbuf1, obuf0, obuf1, gsem.at[0], gsem.at[1], ssem.at[0], ssem.at[1])

    @pl.when(i % 2 == 1)
    def _odd():
        step(xbuf1, xbuf0, obuf1, obuf0, gsem.at[1], gsem.at[0], ssem.at[1], ssem.at[0])


def _moe(x1e, gidx, t, wg_bf, wu_bf, wd_bf, layer, ln_g, ln_b, alpha, d):
    de = wg_bf.shape[3]
    tm = min(ROW_TILE, t)
    src, tg, nvalid = _dispatch_plan(gidx, tm)
    n_tiles = tg.shape[0]
    epg = EXPERTS_PER_GROUP
    full = lambda shape: pl.BlockSpec(shape, lambda i, *_: (0,) * len(shape))
    grid_spec = pltpu.PrefetchScalarGridSpec(
        num_scalar_prefetch=3,
        grid=(n_tiles,),
        in_specs=[pl.BlockSpec(memory_space=pl.ANY),
                  pl.BlockSpec((None, epg, d, de), lambda i, src_, tg_, nv_: (layer, tg_[i], 0, 0)),
                  pl.BlockSpec((None, epg, d, de), lambda i, src_, tg_, nv_: (layer, tg_[i], 0, 0)),
                  pl.BlockSpec((None, epg, de, d), lambda i, src_, tg_, nv_: (layer, tg_[i], 0, 0)),
                  full((1, d)), full((1, d))],
        out_specs=pl.BlockSpec(memory_space=pl.ANY),
        scratch_shapes=[pltpu.VMEM((tm + SUBLANES, d + LANES), F32), pltpu.VMEM((tm + SUBLANES, d + LANES), F32),
                        pltpu.VMEM((tm, d), F32), pltpu.VMEM((tm, d), F32),
                        pltpu.SemaphoreType.DMA((2,)),
                        pltpu.SemaphoreType.DMA((2,))],
    )
    return pl.pallas_call(
        functools.partial(_moe_kernel, alpha=alpha, tm=tm, d=d, n_tiles=n_tiles),
        grid_spec=grid_spec,
        out_shape=jax.ShapeDtypeStruct((t, d), F32),
        compiler_params=_cparams(("arbitrary",)),
        name="moe_ln",
    )(src, tg, nvalid, x1e, wg_bf, wu_bf, wd_bf, ln_g, ln_b)


def kernel(x, w_in, conv_w, conv_b, w_rg_a, b_rg_a, w_rg_i, b_rg_i, lru_lambda, g_attn_norm, g_rec_norm, w_out,
           ln1_g, ln1_b, w_router_group, b_router_group, w_router_expert, b_router_expert, w_gate, w_up, w_down,
           ln2_g, ln2_b):
    b, s, d = x.shape
    depth = w_in.shape[0]
    rw = conv_w.shape[2]
    aw = (w_in.shape[2] - 2 * rw) // 3
    alpha = (2 * depth) ** 0.25
    tables = _rope_tables(s)
    row = lambda v: v.reshape(1, -1)

    wg_bf, wu_bf, wd_bf = w_gate.astype(BF16), w_up.astype(BF16), w_down.astype(BF16)
    x2 = x.reshape(b * s, d)
    for l in range(depth):
        q, k, v, rec = _in_proj(x2, w_in, l, tables, s, aw)
        attn = _moba_attention(q.reshape(b, s, aw), k.reshape(b, s, aw), v.reshape(b, s, aw))
        w_gates = jnp.concatenate([_block_diag(w_rg_a[l]), _block_diag(w_rg_i[l])], axis=1).astype(BF16)
        rec_out = _rglru(rec.reshape(b, s, 2 * rw), conv_w[l], row(conv_b[l]), w_gates,
                         row(b_rg_a[l]), row(b_rg_i[l]), row(lru_lambda[l]))
        n_route = N_GROUPS + N_GROUPS * EXPERTS_PER_GROUP
        w_router = jnp.concatenate([w_router_group[l], w_router_expert[l]], axis=1)
        w_router = jnp.pad(w_router, ((0, 0), (0, LANES - n_route))).astype(BF16)
        b_router = jnp.pad(jnp.concatenate([b_router_group[l], b_router_expert[l]]), (0, LANES - n_route))
        x1e, gidx = _out_proj(attn.reshape(b * s, aw), rec_out.reshape(b * s, rw), x2,
                        row(g_attn_norm[l]), row(g_rec_norm[l]), w_out, l,
                        row(ln1_g[l]), row(ln1_b[l]), w_router, row(b_router), alpha)
        x2 = _moe(x1e, gidx[0].astype(jnp.int32), b * s, wg_bf, wu_bf, wd_bf, l,
                  row(ln2_g[l]), row(ln2_b[l]), alpha, d)
    return x2.reshape(b, s, d)
```
